```python
import jax, jax.numpy as jnp
from jax import lax
import numpy as np

D_MODEL = 1024
BATCH = 4
SEQ = 8192
DEPTH = 1

N_META = 16
HEAD_DIM = 64
SB_HEADS = D_MODEL // (2 * HEAD_DIM)
RW_HEADS = D_MODEL // (2 * HEAD_DIM)
SB_WIDTH = SB_HEADS * HEAD_DIM
RW_WIDTH = RW_HEADS * HEAD_DIM
DECAY_LORA = 64
AAA_LORA = 64
GATE_LORA = 128
RW_SHIFT_WIDTH = 3 * RW_WIDTH + DECAY_LORA + AAA_LORA + GATE_LORA
P_IN = 3 * SB_WIDTH + RW_SHIFT_WIDTH + 2 * D_MODEL
D_FF = -(-8 * D_MODEL // (3 * 256)) * 256
Q_BLOCK = 128
RMS_EPS = 1e-6
GN_EPS = 64e-5

kernel_name = 'hybrid_stickbreak_rwkv7_block'


def rms_norm(x, g):
    xf = x.astype(jnp.float32)
    y = xf * lax.rsqrt(jnp.mean(xf * xf, axis=-1, keepdims=True) + RMS_EPS)
    return (y * g.astype(jnp.float32)).astype(x.dtype)


def token_shift(p, mu):
    prev = jnp.pad(p, ((0, 0), (1, 0), (0, 0)))[:, :-1]
    return p + mu * (prev - p)


def sb_block(q_blk, q_pos, k, v):
    T = k.shape[2]
    z = jnp.einsum('bhqd,bhsd->bhqs', q_blk, k).astype(jnp.float32) * (HEAD_DIM ** -0.5)
    visible = jnp.arange(T)[None, :] < q_pos[:, None]
    neg_log_1m_beta = jnp.where(visible, jax.nn.softplus(z), 0.0)
    between = lax.cumsum(neg_log_1m_beta, axis=3, reverse=True) - neg_log_1m_beta
    weights = jnp.where(visible, jnp.exp(jax.nn.log_sigmoid(z) - between), 0.0)
    return jnp.einsum('bhqs,bhsd->bhqd', weights.astype(v.dtype), v)


def stick_breaking_attention(q, k, v):
    B, T, _ = q.shape
    S = T - N_META
    nb = S // Q_BLOCK
    to_heads = lambda t: t.reshape(B, T, SB_HEADS, HEAD_DIM).transpose(0, 2, 1, 3)
    q, k, v = to_heads(q), to_heads(k), to_heads(v)
    lead = sb_block(q[:, :, :N_META], jnp.arange(N_META), k, v)
    qb = q[:, :, N_META:].reshape(B, SB_HEADS, nb, Q_BLOCK, HEAD_DIM).transpose(2, 0, 1, 3, 4)
    starts = N_META + Q_BLOCK * jnp.arange(nb)
    body = lambda args: sb_block(args[0], args[1] + jnp.arange(Q_BLOCK), k, v)
    out = lax.map(body, (qb, starts))
    out = out.transpose(1, 2, 0, 3, 4).reshape(B, SB_HEADS, S, HEAD_DIM)
    o = jnp.concatenate([lead, out], axis=2)
    return o.transpose(0, 2, 1, 3).reshape(B, T, SB_WIDTH)


def rwkv7_scan(r, decay, k, v, kk, a):
    B, T, H, N = r.shape

    def step(S, inp):
        r_t, w_t, k_t, v_t, kk_t, a_t = inp
        sa = jnp.einsum('bhvk,bhk->bhv', S, -kk_t)
        S = (S * w_t[:, :, None, :] + sa[..., None] * (kk_t * a_t)[:, :, None, :]
             + v_t[..., None] * k_t[:, :, None, :])
        y = jnp.einsum('bhvk,bhk->bhv', S, r_t)
        return S, y

    xs = tuple(jnp.moveaxis(t, 1, 0) for t in (r, decay, k, v, kk, a))
    S0 = jnp.zeros((B, H, N, N), jnp.float32)
    _, ys = lax.scan(step, S0, xs)
    return jnp.moveaxis(ys, 0, 1)


def rwkv7_time_mix(r, k, v, xw, xa, xg, decay_up, decay_base, aaa_up, aaa_base, gate_up,
                   k_k, k_a, r_k, ln_gain, ln_bias):
    B, T, _ = r.shape
    f32 = jnp.float32
    heads = lambda t: t.astype(f32).reshape(B, T, RW_HEADS, HEAD_DIM)
    w_log = -jax.nn.softplus(-(decay_base + jnp.tanh(xw) @ decay_up)) - 0.5
    decay = jnp.exp(-jnp.exp(w_log.astype(f32)))
    a = jax.nn.sigmoid(aaa_base + xa @ aaa_up)
    g = jax.nn.sigmoid(xg) @ gate_up
    kk = heads(k * k_k)
    kk = kk / jnp.maximum(jnp.sqrt(jnp.sum(kk * kk, axis=-1, keepdims=True)), 1e-12)
    k = k * (1.0 + (a - 1.0) * k_a)
    rh, kh, vh, ah = heads(r), heads(k), heads(v), heads(a)
    y = rwkv7_scan(rh, heads(decay), kh, vh, kk, ah)
    mu = jnp.mean(y, axis=-1, keepdims=True)
    var = jnp.mean(jnp.square(y - mu), axis=-1, keepdims=True)
    y = ((y - mu) * lax.rsqrt(var + GN_EPS)).reshape(B, T, RW_WIDTH)
    y = y * ln_gain.astype(f32) + ln_bias.astype(f32)
    bonus = jnp.sum(rh * kh * r_k.astype(f32), axis=-1, keepdims=True) * vh
    y = (y + bonus.reshape(B, T, RW_WIDTH)) * g.astype(f32)
    return y.astype(r.dtype)


def setup_inputs(seed: int = 0) -> dict:
    key = jax.random.key(seed)
    ks = jax.random.split(key, 26)
    nrm = lambda k, shape, s: jax.random.normal(k, shape, jnp.float32) * s
    L = DEPTH
    return {
        'x': nrm(ks[0], (BATCH, SEQ, D_MODEL), 1.0),
        'meta_tokens': nrm(ks[1], (N_META, D_MODEL), 1.0),
        'norm_mix_pre': 1.0 + nrm(ks[2], (L, D_MODEL), 0.05),
        'norm_mix_post': 1.0 + nrm(ks[3], (L, D_MODEL), 0.05),
        'w_in': nrm(ks[4], (L, D_MODEL, P_IN), D_MODEL ** -0.5),
        'rw_shift_mu': jax.random.uniform(ks[5], (L, RW_SHIFT_WIDTH), jnp.float32),
        'rw_decay_up': nrm(ks[6], (L, DECAY_LORA, RW_WIDTH), 0.5 * DECAY_LORA ** -0.5),
        'rw_decay_base': jax.random.uniform(ks[7], (L, RW_WIDTH), jnp.float32, -6.0, -1.0),
        'rw_aaa_up': nrm(ks[8], (L, AAA_LORA, RW_WIDTH), AAA_LORA ** -0.5),
        'rw_aaa_base': nrm(ks[9], (L, RW_WIDTH), 0.1),
        'rw_gate_up': nrm(ks[10], (L, GATE_LORA, RW_WIDTH), GATE_LORA ** -0.5),
        'rw_k_k': 0.85 + nrm(ks[11], (L, RW_WIDTH), 0.1),
        'rw_k_a': 1.0 + nrm(ks[12], (L, RW_WIDTH), 0.1),
        'rw_r_k': nrm(ks[13], (L, RW_HEADS, HEAD_DIM), 0.1),
        'rw_ln_gain': 1.0 + nrm(ks[14], (L, RW_WIDTH), 0.05),
        'rw_ln_bias': nrm(ks[15], (L, RW_WIDTH), 0.02),
        'w_branch_sb': nrm(ks[16], (L, SB_WIDTH, D_MODEL), SB_WIDTH ** -0.5),
        'w_branch_rw': nrm(ks[17], (L, RW_WIDTH, D_MODEL), RW_WIDTH ** -0.5),
        'w_out': nrm(ks[18], (L, D_MODEL, D_MODEL), D_MODEL ** -0.5),
        'norm_ffn_pre': 1.0 + nrm(ks[19], (L, D_MODEL), 0.05),
        'norm_ffn_post': 1.0 + nrm(ks[20], (L, D_MODEL), 0.05),
        'w_ffn_gate': nrm(ks[21], (L, D_MODEL, D_FF), D_MODEL ** -0.5),
        'w_ffn_up': nrm(ks[22], (L, D_MODEL, D_FF), D_MODEL ** -0.5),
        'w_ffn_down': nrm(ks[23], (L, D_FF, D_MODEL), D_FF ** -0.5),
    }


def reference(x, meta_tokens, norm_mix_pre, norm_mix_post, w_in, rw_shift_mu, rw_decay_up,
              rw_decay_base, rw_aaa_up, rw_aaa_base, rw_gate_up, rw_k_k, rw_k_a, rw_r_k,
              rw_ln_gain, rw_ln_bias, w_branch_sb, w_branch_rw, w_out, norm_ffn_pre,
              norm_ffn_post, w_ffn_gate, w_ffn_up, w_ffn_down):
    B = x.shape[0]
    meta = jnp.broadcast_to(meta_tokens[None].astype(x.dtype), (B, N_META, D_MODEL))
    h = jnp.concatenate([meta, x], axis=1)
    cut = [3 * SB_WIDTH, 3 * SB_WIDTH + RW_SHIFT_WIDTH, 3 * SB_WIDTH + RW_SHIFT_WIDTH + D_MODEL]
    rw_cut = [RW_WIDTH, 2 * RW_WIDTH, 3 * RW_WIDTH, 3 * RW_WIDTH + DECAY_LORA,
              3 * RW_WIDTH + DECAY_LORA + AAA_LORA]
    for l in range(DEPTH):
        u = rms_norm(h, norm_mix_pre[l]) @ w_in[l]
        u_sb, u_rw, gate_sb, gate_rw = jnp.split(u, cut, axis=-1)
        sq, sk, sv = jnp.split(u_sb, 3, axis=-1)
        o_sb = stick_breaking_attention(sq, sk, sv)
        u_rw = token_shift(u_rw, rw_shift_mu[l])
        rr, rk, rv, rxw, rxa, rxg = jnp.split(u_rw, rw_cut, axis=-1)
        o_rw = rwkv7_time_mix(rr, rk, rv, rxw, rxa, rxg, rw_decay_up[l], rw_decay_base[l],
                              rw_aaa_up[l], rw_aaa_base[l], rw_gate_up[l], rw_k_k[l], rw_k_a[l],
                              rw_r_k[l], rw_ln_gain[l], rw_ln_bias[l])
        merged = (jax.nn.sigmoid(gate_sb) * (o_sb @ w_branch_sb[l])
                  + jax.nn.sigmoid(gate_rw) * (o_rw @ w_branch_rw[l]))
        h = h + rms_norm(merged @ w_out[l], norm_mix_post[l])
        f = rms_norm(h, norm_ffn_pre[l])
        f = (jax.nn.silu(f @ w_ffn_gate[l]) * (f @ w_ffn_up[l])) @ w_ffn_down[l]
        h = h + rms_norm(f, norm_ffn_post[l])
    return h[:, N_META:]
```

```python
import functools

import jax
import jax.numpy as jnp
from jax import lax
from jax.experimental import pallas as pl
from jax.experimental.pallas import tpu as pltpu

HEAD_DIM = 64
RMS_EPS = 1e-6
GN_EPS = 64e-5
KK_EPS = 1e-12
LANES = 128
CHUNK = 64
VMEM_LIMIT = 48 * 1024 * 1024

F32 = jnp.float32
BF16 = jnp.bfloat16

NN = (((1,), (0,)), ((), ()))
NT = (((1,), (1,)), ((), ()))
TN = (((0,), (0,)), ((), ()))


def _split(x, n):
    if x.dtype == BF16:
        return [x]
    parts = []
    r = x
    for i in range(n):
        p = r.astype(BF16)
        parts.append(p)
        if i + 1 < n:
            r = r - p.astype(F32)
    return parts


def _mm(a, b, dims=NN, pa=1, pb=1):
    a_parts = _split(a, pa)
    b_parts = _split(b, pb)
    order = max(len(a_parts), len(b_parts))
    acc = None
    for i, ai in enumerate(a_parts):
        for j, bj in enumerate(b_parts):
            if i + j >= order:
                continue
            t = lax.dot_general(ai, bj, dims, preferred_element_type=F32)
            acc = t if acc is None else acc + t
    return acc


def _softplus(z):
    return jnp.maximum(z, 0.0) + jnp.log1p(jnp.exp(-jnp.abs(z)))


def _rms(x, g):
    ms = jnp.mean(x * x, axis=-1, keepdims=True)
    return x * lax.rsqrt(ms + RMS_EPS) * g


def _proj_kernel(x_ref, g_ref, w_ref, o_ref, xn_ref, *, act):
    @pl.when(pl.program_id(1) == 0)
    def _():
        xn_ref[...] = _rms(x_ref[...], g_ref[...]).astype(BF16)

    y = jnp.dot(xn_ref[...], w_ref[...], preferred_element_type=F32)
    if act == "sigmoid":
        y = jax.nn.sigmoid(y)
    o_ref[...] = y.astype(o_ref.dtype)


def _proj(x2d, gain, w, out_dtype, act, tm, tn, name):
    m, d = x2d.shape
    n = w.shape[1]
    assert m % tm == 0 and n % tn == 0
    return pl.pallas_call(
        functools.partial(_proj_kernel, act=act),
        out_shape=jax.ShapeDtypeStruct((m, n), out_dtype),
        grid=(m // tm, n // tn),
        in_specs=[
            pl.BlockSpec((tm, d), lambda i, j: (i, 0)),
            pl.BlockSpec((1, d), lambda i, j: (0, 0)),
            pl.BlockSpec((d, tn), lambda i, j: (0, j)),
        ],
        out_specs=pl.BlockSpec((tm, tn), lambda i, j: (i, j)),
        scratch_shapes=[pltpu.VMEM((tm, d), BF16)],
        compiler_params=pltpu.CompilerParams(
            dimension_semantics=("parallel", "arbitrary"), vmem_limit_bytes=VMEM_LIMIT),
        name=name,
    )(x2d, gain, w)


def _sb_kernel(q_ref, k_ref, v_ref, km_ref, vm_ref, tri_ref, o_ref, *, tq, n_meta):
    qi = pl.program_id(2)
    lane = lax.broadcasted_iota(jnp.int32, (tq, LANES), 1)
    q2 = q_ref[0] * jnp.asarray(HEAD_DIM ** -0.5, BF16)
    row = lax.broadcasted_iota(jnp.int32, (tq, tq), 0)
    col = lax.broadcasted_iota(jnp.int32, (tq, tq), 1)
    diag_mask = col < row
    colm = lax.broadcasted_iota(jnp.int32, (tq, LANES), 1)
    meta_mask = colm < n_meta
    tri = tri_ref[...]
    tri_m = tri_ref[0:LANES, 0:LANES]

    def block(qh, kblk, vblk, carry, acc, mask, tri_b):
        z = lax.dot_general(qh, kblk, NT, preferred_element_type=F32)
        sp = _softplus(z)
        if mask is not None:
            sp = jnp.where(mask, sp, 0.0)
        c = jnp.dot(sp.astype(BF16), tri_b, preferred_element_type=F32) + carry
        w = jnp.exp(z - c)
        if mask is not None:
            w = jnp.where(mask, w, 0.0)
        acc = acc + jnp.dot(w.astype(BF16), vblk, preferred_element_type=F32)
        return c[:, 0:1], acc

    outs = []
    for hh in range(2):
        head_mask = (lane < HEAD_DIM) if hh == 0 else (lane >= HEAD_DIM)
        qh = jnp.where(head_mask, q2, jnp.zeros_like(q2))
        carry = jnp.zeros((tq, 1), F32)
        acc = jnp.zeros((tq, LANES), F32)
        d0 = pl.multiple_of(qi * tq, tq)
        carry, acc = block(qh, k_ref[0, pl.ds(d0, tq), :], v_ref[0, pl.ds(d0, tq), :],
                           carry, acc, diag_mask, tri)

        def body(i, state, qh=qh):
            carry, acc = state
            s0 = pl.multiple_of((qi - 1 - i) * tq, tq)
            return block(qh, k_ref[0, pl.ds(s0, tq), :], v_ref[0, pl.ds(s0, tq), :],
                         carry, acc, None, tri)

        carry, acc = lax.fori_loop(0, qi, body, (carry, acc))
        carry, acc = block(qh, km_ref[...], vm_ref[...], carry, acc, meta_mask, tri_m)
        outs.append(acc)
    o_ref[0] = jnp.where(lane < HEAD_DIM, outs[0], outs[1]).astype(o_ref.dtype)


def _sb_attn(qkv, qkv_meta, n_meta, tq):
    b, s, w3 = qkv.shape
    width = w3 // 3
    pairs = width // LANES
    assert s % tq == 0 and tq % LANES == 0
    tri = (jnp.arange(tq)[:, None] >= jnp.arange(tq)[None, :]).astype(BF16)
    return pl.pallas_call(
        functools.partial(_sb_kernel, tq=tq, n_meta=n_meta),
        out_shape=jax.ShapeDtypeStruct((b, s, width), BF16),
        grid=(b, pairs, s // tq),
        in_specs=[
            pl.BlockSpec((1, tq, LANES), lambda bi, p, qi: (bi, qi, p)),
            pl.BlockSpec((1, s, LANES), lambda bi, p, qi: (bi, 0, pairs + p)),
            pl.BlockSpec((1, s, LANES), lambda bi, p, qi: (bi, 0, 2 * pairs + p)),
            pl.BlockSpec((LANES, LANES), lambda bi, p, qi: (0, pairs + p)),
            pl.BlockSpec((LANES, LANES), lambda bi, p, qi: (0, 2 * pairs + p)),
            pl.BlockSpec((tq, tq), lambda bi, p, qi: (0, 0)),
        ],
        out_specs=pl.BlockSpec((1, tq, LANES), lambda bi, p, qi: (bi, qi, p)),
        compiler_params=pltpu.CompilerParams(
            dimension_semantics=("parallel", "parallel", "arbitrary"),
            vmem_limit_bytes=VMEM_LIMIT),
        name="sb_attn",
    )(qkv, qkv, qkv, qkv_meta, qkv_meta, tri)


P_LORA = 2
P_EXACT = 3
P_INTRA = 2
P_INV = 2
P_STATE = 2


def _rwkv_kernel(ux_ref, um_ref, mu_ref, wlo_ref, dbase_ref, abase_ref, gup_ref, kk_ref, ka_ref,
                 rk_ref, lng_ref, lnb_ref, bd_ref, tril_ref, o_ref, st_ref, carry_ref, y_ref,
                 *, width):
    c = pl.program_id(1)
    heads = width // HEAD_DIM
    C = CHUNK

    @pl.when(c == 0)
    def _():
        st_ref[...] = jnp.zeros_like(st_ref)
        carry_ref[...] = jnp.zeros_like(carry_ref)

    u = jnp.where(c == 0, um_ref[...], ux_ref[0])
    rowi = lax.broadcasted_iota(jnp.int32, u.shape, 0)
    prev = jnp.where(rowi == 0, carry_ref[...], pltpu.roll(u, 1, axis=0))
    carry_ref[...] = u[C - 1:C, :]
    us = u + mu_ref[...] * (prev - u)

    rr = us[:, 0:width]
    rk = us[:, width:2 * width]
    rv = us[:, 2 * width:3 * width]
    xwa = us[:, 3 * width:3 * width + LANES]
    xg = us[:, 3 * width + LANES:3 * width + 2 * LANES]

    lane_wa = lax.broadcasted_iota(jnp.int32, xwa.shape, 1)
    twa = jnp.where(lane_wa < HEAD_DIM, jnp.tanh(xwa), xwa)
    lo = _mm(twa, wlo_ref[...], pa=P_LORA, pb=P_LORA)
    dec_pre = dbase_ref[...] + lo[:, 0:width]
    ld = -jnp.exp(-_softplus(-dec_pre) - 0.5)
    a = jax.nn.sigmoid(abase_ref[...] + lo[:, width:2 * width])
    g = _mm(jax.nn.sigmoid(xg), gup_ref[...], pa=P_LORA, pb=P_LORA)

    bd = bd_ref[...]
    kkr = rk * kk_ref[...]
    ss = _mm(kkr * kkr, bd, pa=P_EXACT)
    kk = kkr / jnp.maximum(jnp.sqrt(ss), KK_EPS)
    k = rk * (1.0 + (a - 1.0) * ka_ref[...])
    beta = kk * a

    cs = _mm(tril_ref[...], ld, pb=P_EXACT)
    cs_last = cs[C - 1:C, :]
    e_in = jnp.exp(cs)
    e_ex = jnp.exp(cs - ld)
    e_neg = jnp.exp(-cs)
    e_last = jnp.exp(cs_last - cs)
    d_last = jnp.exp(cs_last)
    a_t = -kk * e_ex
    r_t = rr * e_in
    b_t = beta * e_neg
    k_t = k * e_neg
    b_h = beta * e_last
    k_h = k * e_last

    ri = lax.broadcasted_iota(jnp.int32, (C, C), 0)
    ci = lax.broadcasted_iota(jnp.int32, (C, C), 1)
    strict = ci < ri
    incl = ci <= ri
    eye = ci == ri

    for h in range(heads):
        sl = slice(h * HEAD_DIM, (h + 1) * HEAD_DIM)
        A, R, Bm, K, V = a_t[:, sl], r_t[:, sl], b_t[:, sl], k_t[:, sl], rv[:, sl]
        Bh, Kh = b_h[:, sl], k_h[:, sl]
        l_ab = jnp.where(strict, _mm(A, Bm, NT, P_INTRA, P_INTRA), 0.0)
        l_ak = jnp.where(strict, _mm(A, K, NT, P_INTRA, P_INTRA), 0.0)
        p_rb = jnp.where(incl, _mm(R, Bm, NT, P_INTRA, P_INTRA), 0.0)
        p_rk = jnp.where(incl, _mm(R, K, NT, P_INTRA, P_INTRA), 0.0)
        t_inv = jnp.where(eye, 1.0, l_ab)
        pw = l_ab
        for _ in range(5):
            pw = _mm(pw, pw, NN, P_INV, P_INV)
            t_inv = t_inv + _mm(t_inv, pw, NN, P_INV, P_INV)
        lv = _mm(l_ak, V, NN, P_INTRA, P_INTRA)
        Wm = _mm(t_inv, A, NN, P_INTRA, P_INTRA)
        U = _mm(t_inv, lv, NN, P_INTRA, P_INTRA)
        ya = R + _mm(p_rb, Wm, NN, P_INTRA, P_INTRA)
        yb = _mm(p_rb, U, NN, P_INTRA, P_INTRA) + _mm(p_rk, V, NN, P_INTRA, P_INTRA)
        G = jnp.where(eye, jnp.broadcast_to(d_last[:, sl], (C, C)), 0.0) + _mm(Bh, Wm, TN, P_INTRA, P_INTRA)
        H = _mm(Bh, U, TN, P_INTRA, P_INTRA) + _mm(Kh, V, TN, P_INTRA, P_INTRA)
        m0 = st_ref[h]
        y_ref[:, sl] = _mm(ya, m0, NN, P_STATE, P_STATE) + yb
        st_ref[h] = _mm(G, m0, NN, P_STATE, P_STATE) + H

    y = y_ref[...]
    inv_n = 1.0 / HEAD_DIM
    mean = _mm(y, bd, pa=P_EXACT) * inv_n
    yc = y - mean
    var = _mm(yc * yc, bd, pa=P_EXACT) * inv_n
    yn = yc * lax.rsqrt(var + GN_EPS) * lng_ref[...] + lnb_ref[...]
    bonus = _mm(rr * k * rk_ref[...], bd, pa=P_EXACT) * rv
    o_ref[0] = ((yn + bonus) * g).astype(o_ref.dtype)


def _rwkv(u_x, u_meta, mu, w_lora, dbase, abase, gate_up, k_k, k_a, r_k, ln_g, ln_b):
    b, s, uw = u_x.shape
    width = dbase.shape[1]
    assert s % CHUNK == 0 and uw == 3 * width + 2 * LANES
    nc = s // CHUNK + 1
    hid = jnp.arange(width) // HEAD_DIM
    bd = (hid[:, None] == hid[None, :]).astype(BF16)
    tril = (jnp.arange(CHUNK)[:, None] >= jnp.arange(CHUNK)[None, :]).astype(BF16)
    const = lambda shape: pl.BlockSpec(shape, lambda bi, c: tuple(0 for _ in shape))
    return pl.pallas_call(
        functools.partial(_rwkv_kernel, width=width),
        out_shape=jax.ShapeDtypeStruct((b, s, width), BF16),
        grid=(b, nc),
        in_specs=[
            pl.BlockSpec((1, CHUNK, uw), lambda bi, c: (bi, jnp.maximum(c - 1, 0), 0)),
            const((CHUNK, uw)), const((1, uw)), const((LANES, 2 * width)), const((1, width)),
            const((1, width)), const((LANES, width)), const((1, width)), const((1, width)),
            const((1, width)), const((1, width)), const((1, width)), const((width, width)),
            const((CHUNK, CHUNK)),
        ],
        out_specs=pl.BlockSpec((1, CHUNK, width), lambda bi, c: (bi, jnp.maximum(c - 1, 0), 0)),
        scratch_shapes=[
            pltpu.VMEM((width // HEAD_DIM, HEAD_DIM, HEAD_DIM), F32),
            pltpu.VMEM((1, uw), F32),
            pltpu.VMEM((CHUNK, width), F32),
        ],
        compiler_params=pltpu.CompilerParams(
            dimension_semantics=("parallel", "arbitrary"), vmem_limit_bytes=VMEM_LIMIT),
        name="rwkv",
    )(u_x, u_meta, mu, w_lora, dbase, abase, gate_up, k_k, k_a, r_k, ln_g, ln_b, bd, tril)


def _merge_kernel(x_ref, osb_ref, orw_ref, gsb_ref, grw_ref, wsb_ref, wrw_ref, wout_ref, g_ref, o_ref):
    a = jnp.dot(osb_ref[...], wsb_ref[...], preferred_element_type=F32)
    b = jnp.dot(orw_ref[...], wrw_ref[...], preferred_element_type=F32)
    m = gsb_ref[...].astype(F32) * a + grw_ref[...].astype(F32) * b
    y = jnp.dot(m.astype(BF16), wout_ref[...], preferred_element_type=F32)
    o_ref[...] = x_ref[...] + _rms(y, g_ref[...])


def _merge(x2d, o_sb, o_rw, gates, w_sb, w_rw, w_out, gain, tm):
    m, d = x2d.shape
    width = o_sb.shape[1]
    assert m % tm == 0
    row = lambda cols, jb=0: pl.BlockSpec((tm, cols), lambda i: (i, jb))
    full = lambda shape: pl.BlockSpec(shape, lambda i: (0, 0))
    return pl.pallas_call(
        _merge_kernel,
        out_shape=jax.ShapeDtypeStruct((m, d), F32),
        grid=(m // tm,),
        in_specs=[row(d), row(width), row(width), row(d, 0), row(d, 1),
                  full((width, d)), full((width, d)), full((d, d)), full((1, d))],
        out_specs=row(d),
        compiler_params=pltpu.CompilerParams(
            dimension_semantics=("parallel",), vmem_limit_bytes=VMEM_LIMIT),
        name="merge",
    )(x2d, o_sb, o_rw, gates, gates, w_sb, w_rw, w_out, gain)


def _ffn_kernel(h_ref, gpre_ref, wg_ref, wu_ref, wd_ref, gpost_ref, o_ref, xn_ref, acc_ref):
    j = pl.program_id(1)

    @pl.when(j == 0)
    def _():
        xn_ref[...] = _rms(h_ref[...], gpre_ref[...]).astype(BF16)
        acc_ref[...] = jnp.zeros_like(acc_ref)

    xn = xn_ref[...]
    gate = jnp.dot(xn, wg_ref[...], preferred_element_type=F32)
    up = jnp.dot(xn, wu_ref[...], preferred_element_type=F32)
    act = (gate * jax.nn.sigmoid(gate) * up).astype(BF16)
    acc_ref[...] += jnp.dot(act, wd_ref[...], preferred_element_type=F32)

    @pl.when(j == pl.num_programs(1) - 1)
    def _():
        o_ref[...] = h_ref[...] + _rms(acc_ref[...], gpost_ref[...])


def _ffn(h2d, g_pre, w_gate, w_up, w_down, g_post, tm, tf):
    m, d = h2d.shape
    ff = w_gate.shape[1]
    assert m % tm == 0 and ff % tf == 0
    return pl.pallas_call(
        _ffn_kernel,
        out_shape=jax.ShapeDtypeStruct((m, d), F32),
        grid=(m // tm, ff // tf),
        in_specs=[
            pl.BlockSpec((tm, d), lambda i, j: (i, 0)),
            pl.BlockSpec((1, d), lambda i, j: (0, 0)),
            pl.BlockSpec((d, tf), lambda i, j: (0, j)),
            pl.BlockSpec((d, tf), lambda i, j: (0, j)),
            pl.BlockSpec((tf, d), lambda i, j: (j, 0)),
            pl.BlockSpec((1, d), lambda i, j: (0, 0)),
        ],
        out_specs=pl.BlockSpec((tm, d), lambda i, j: (i, 0)),
        scratch_shapes=[pltpu.VMEM((tm, d), BF16), pltpu.VMEM((tm, d), F32)],
        compiler_params=pltpu.CompilerParams(
            dimension_semantics=("parallel", "arbitrary"), vmem_limit_bytes=VMEM_LIMIT),
        name="ffn",
    )(h2d, g_pre, w_gate, w_up, w_down, g_post)


def _pick(n, prefs):
    for t in prefs:
        if n % t == 0:
            return t
    return n


def kernel(x, meta_tokens, norm_mix_pre, norm_mix_post, w_in, rw_shift_mu, rw_decay_up, rw_decay_base,
           rw_aaa_up, rw_aaa_base, rw_gate_up, rw_k_k, rw_k_a, rw_r_k, rw_ln_gain, rw_ln_bias,
           w_branch_sb, w_branch_rw, w_out, norm_ffn_pre, norm_ffn_post, w_ffn_gate, w_ffn_up,
           w_ffn_down):
    b, s, d = x.shape
    n_meta = meta_tokens.shape[0]
    depth = w_in.shape[0]
    assert depth == 1, "meta rows are only carried through the mixer of a single layer"
    width = w_branch_sb.shape[1]
    dlora = rw_decay_up.shape[1]
    alora = rw_aaa_up.shape[1]
    glora = rw_gate_up.shape[1]
    assert dlora + alora == LANES and glora == LANES and n_meta <= CHUNK
    l = 0
    m = b * s
    x2d = x.reshape(m, d)

    c_sb, c_rw = 3 * width, 3 * width + 3 * width + 2 * LANES
    w_in_b = w_in[l].astype(BF16)
    w_qkv, w_rw, w_gates = w_in_b[:, :c_sb], w_in_b[:, c_sb:c_rw], w_in_b[:, c_rw:]
    g_pre = norm_mix_pre[l][None, :]

    tm = _pick(m, (1024, 512, 256, 128))
    qkv = _proj(x2d, g_pre, w_qkv, BF16, None, tm, _pick(c_sb, (768, 512, 256, 128)), "proj_qkv")
    u_rw = _proj(x2d, g_pre, w_rw, F32, None, tm, _pick(c_rw - c_sb, (896, 512, 256, 128)), "proj_rw")
    gates = _proj(x2d, g_pre, w_gates, BF16, "sigmoid", tm, _pick(2 * d, (1024, 512, 256, 128)),
                  "proj_gates")
    meta_pad = jnp.zeros((LANES, d), F32).at[:n_meta].set(meta_tokens.astype(F32))
    qkv_meta = _proj(meta_pad, g_pre, w_qkv, BF16, None, LANES, _pick(c_sb, (768, 512, 256, 128)),
                     "proj_qkv_meta")
    rw_meta = _proj(meta_pad, g_pre, w_rw, F32, None, LANES, _pick(c_rw - c_sb, (896, 512, 256, 128)),
                    "proj_rw_meta")
    row_ok = (jnp.arange(LANES) < n_meta)[:, None]
    qkv_meta = jnp.where(row_ok, qkv_meta, jnp.zeros_like(qkv_meta))
    u_meta = jnp.zeros((CHUNK, c_rw - c_sb), F32).at[CHUNK - n_meta:].set(rw_meta[:n_meta])

    o_sb = _sb_attn(qkv.reshape(b, s, c_sb), qkv_meta, n_meta, _pick(s, (256, 128)))

    w_lora = jnp.zeros((LANES, 2 * width), F32)
    w_lora = w_lora.at[:dlora, :width].set(rw_decay_up[l]).at[dlora:, width:].set(rw_aaa_up[l])
    vec = lambda p: p[l].reshape(1, -1).astype(F32)
    o_rw = _rwkv(u_rw.reshape(b, s, c_rw - c_sb), u_meta, vec(rw_shift_mu), w_lora,
                 vec(rw_decay_base), vec(rw_aaa_base), rw_gate_up[l].astype(F32), vec(rw_k_k),
                 vec(rw_k_a), vec(rw_r_k), vec(rw_ln_gain), vec(rw_ln_bias))

    h1 = _merge(x2d, o_sb.reshape(m, width), o_rw.reshape(m, width), gates,
                w_branch_sb[l].astype(BF16), w_branch_rw[l].astype(BF16), w_out[l].astype(BF16),
                norm_mix_post[l][None, :], _pick(m, (512, 256, 128)))
    ff = w_ffn_gate.shape[2]
    out = _ffn(h1, norm_ffn_pre[l][None, :], w_ffn_gate[l].astype(BF16), w_ffn_up[l].astype(BF16),
               w_ffn_down[l].astype(BF16), norm_ffn_post[l][None, :], _pick(m, (512, 256, 128)),
               _pick(ff, (1408, 512, 256, 128)))
    return out.reshape(b, s, d)
```

```python
import functools

import jax
import jax.numpy as jnp
from jax import lax
from jax.experimental import pallas as pl
from jax.experimental.pallas import tpu as pltpu

HEAD_DIM = 64
RMS_EPS = 1e-6
GN_EPS = 64e-5
KK_EPS = 1e-12
LANES = 128
CHUNK = 64
RW_ROWS = 256
VMEM_LIMIT = 48 * 1024 * 1024

F32 = jnp.float32
BF16 = jnp.bfloat16

NN = (((1,), (0,)), ((), ()))
NT = (((1,), (1,)), ((), ()))
TN = (((0,), (0,)), ((), ()))


def _split(x, n):
    if x.dtype == BF16:
        return [x]
    parts = []
    r = x
    for i in range(n):
        p = r.astype(BF16)
        parts.append(p)
        if i + 1 < n:
            r = r - p.astype(F32)
    return parts


def _mm(a, b, dims=NN, pa=1, pb=1):
    a_parts = _split(a, pa)
    b_parts = _split(b, pb)
    order = max(len(a_parts), len(b_parts))
    acc = None
    for i, ai in enumerate(a_parts):
        for j, bj in enumerate(b_parts):
            if i + j >= order:
                continue
            t = lax.dot_general(ai, bj, dims, preferred_element_type=F32)
            acc = t if acc is None else acc + t
    return acc


def _bdot(a, b, dims=NN):
    return lax.dot_general(a.astype(BF16), b.astype(BF16), dims, preferred_element_type=F32)


def _softplus(z):
    return jnp.maximum(z, 0.0) + jnp.log1p(jnp.exp(-jnp.abs(z)))


def _rms(x, g):
    ms = jnp.mean(x * x, axis=-1, keepdims=True)
    return x * lax.rsqrt(ms + RMS_EPS) * g


def _proj_kernel(x_ref, g_ref, w_ref, o_ref, xn_ref, *, act):
    @pl.when(pl.program_id(1) == 0)
    def _():
        xn_ref[...] = _rms(x_ref[...], g_ref[...]).astype(BF16)

    y = jnp.dot(xn_ref[...], w_ref[...], preferred_element_type=F32)
    if act == "sigmoid":
        y = jax.nn.sigmoid(y)
    o_ref[...] = y.astype(o_ref.dtype)


def _proj(x2d, gain, w, out_dtype, act, tm, tn, name):
    m, d = x2d.shape
    n = w.shape[1]
    assert m % tm == 0 and n % tn == 0
    return pl.pallas_call(
        functools.partial(_proj_kernel, act=act),
        out_shape=jax.ShapeDtypeStruct((m, n), out_dtype),
        grid=(m // tm, n // tn),
        in_specs=[
            pl.BlockSpec((tm, d), lambda i, j: (i, 0)),
            pl.BlockSpec((1, d), lambda i, j: (0, 0)),
            pl.BlockSpec((d, tn), lambda i, j: (0, j)),
        ],
        out_specs=pl.BlockSpec((tm, tn), lambda i, j: (i, j)),
        scratch_shapes=[pltpu.VMEM((tm, d), BF16)],
        compiler_params=pltpu.CompilerParams(
            dimension_semantics=("parallel", "arbitrary"), vmem_limit_bytes=VMEM_LIMIT),
        name=name,
    )(x2d, gain, w)


def _sb_kernel(q_ref, k_ref, v_ref, km_ref, vm_ref, tri_ref, o_ref, acc_ref, *, tq, sub, n_meta):
    qi = pl.program_id(2)
    lane = lax.broadcasted_iota(jnp.int32, (tq, LANES), 1)
    q2 = q_ref[0] * jnp.asarray(HEAD_DIM ** -0.5, BF16)
    zero = jnp.zeros_like(q2)
    qh = (jnp.where(lane < HEAD_DIM, q2, zero), jnp.where(lane >= HEAD_DIM, q2, zero))
    tri = tri_ref[...]

    def block(h, kblk, vblk, carry, mask, tri_b, sub_b):
        z = lax.dot_general(qh[h], kblk, NT, preferred_element_type=F32)
        sp = jnp.maximum(z, 0.0) + jnp.log(1.0 + jnp.exp(-jnp.abs(z)))
        if mask is not None:
            sp = jnp.where(mask, sp, 0.0)
        spb = sp.astype(BF16)
        parts = []
        for j in reversed(range(z.shape[1] // sub_b)):
            cj = jnp.dot(spb[:, j * sub_b:(j + 1) * sub_b], tri_b, preferred_element_type=F32) + carry
            carry = cj[:, 0:1]
            parts.insert(0, cj)
        c = parts[0] if len(parts) == 1 else jnp.concatenate(parts, axis=1)
        w = jnp.exp(z - c)
        if mask is not None:
            w = jnp.where(mask, w, 0.0)
        acc_ref[h] += jnp.dot(w.astype(BF16), vblk, preferred_element_type=F32)
        return carry

    acc_ref[...] = jnp.zeros_like(acc_ref)
    row = lax.broadcasted_iota(jnp.int32, (tq, tq), 0)
    col = lax.broadcasted_iota(jnp.int32, (tq, tq), 1)
    diag_mask = col < row
    d0 = pl.multiple_of(qi * tq, tq)
    kd, vd = k_ref[0, pl.ds(d0, tq), :], v_ref[0, pl.ds(d0, tq), :]
    carries = tuple(block(h, kd, vd, jnp.zeros((tq, 1), F32), diag_mask, tri, sub) for h in range(2))

    def body(i, carries):
        s0 = pl.multiple_of((qi - 1 - i) * tq, tq)
        kb, vb = k_ref[0, pl.ds(s0, tq), :], v_ref[0, pl.ds(s0, tq), :]
        return tuple(block(h, kb, vb, carries[h], None, tri, sub) for h in range(2))

    carries = lax.fori_loop(0, qi, body, carries)
    meta_mask = lane < n_meta
    for h in range(2):
        block(h, km_ref[...], vm_ref[...], carries[h], meta_mask, tri_ref[0:LANES, 0:LANES], LANES)
    o_ref[0] = jnp.where(lane < HEAD_DIM, acc_ref[0], acc_ref[1]).astype(o_ref.dtype)


def _sb_attn(qkv, qkv_meta, n_meta, tq, sub):
    b, s, w3 = qkv.shape
    width = w3 // 3
    pairs = width // LANES
    assert s % tq == 0 and tq % sub == 0 and sub % LANES == 0
    tri = (jnp.arange(sub)[:, None] >= jnp.arange(sub)[None, :]).astype(BF16)
    return pl.pallas_call(
        functools.partial(_sb_kernel, tq=tq, sub=sub, n_meta=n_meta),
        out_shape=jax.ShapeDtypeStruct((b, s, width), BF16),
        grid=(b, pairs, s // tq),
        in_specs=[
            pl.BlockSpec((1, tq, LANES), lambda bi, p, qi: (bi, qi, p)),
            pl.BlockSpec((1, s, LANES), lambda bi, p, qi: (bi, 0, pairs + p)),
            pl.BlockSpec((1, s, LANES), lambda bi, p, qi: (bi, 0, 2 * pairs + p)),
            pl.BlockSpec((LANES, LANES), lambda bi, p, qi: (0, pairs + p)),
            pl.BlockSpec((LANES, LANES), lambda bi, p, qi: (0, 2 * pairs + p)),
            pl.BlockSpec((sub, sub), lambda bi, p, qi: (0, 0)),
        ],
        out_specs=pl.BlockSpec((1, tq, LANES), lambda bi, p, qi: (bi, qi, p)),
        scratch_shapes=[pltpu.VMEM((2, tq, LANES), F32)],
        compiler_params=pltpu.CompilerParams(
            dimension_semantics=("parallel", "parallel", "arbitrary"),
            vmem_limit_bytes=VMEM_LIMIT),
        name="sb_attn",
    )(qkv, qkv, qkv, qkv_meta, qkv_meta, tri)


P_LORA = 2
P_EXACT = 3


def _rwkv_kernel(ux_ref, um_ref, mu_ref, wlo_ref, dbase_ref, abase_ref, gup_ref, kk_ref, ka_ref,
                 rk_ref, lng_ref, lnb_ref, bd_ref, tril_ref, o_ref, st_ref, carry_ref, y_ref,
                 *, width):
    c = pl.program_id(1)
    R, C = RW_ROWS, CHUNK
    n_chunks = R // C
    pairs = width // LANES

    @pl.when(c == 0)
    def _():
        st_ref[...] = jnp.zeros_like(st_ref)
        carry_ref[...] = jnp.zeros_like(carry_ref)

    u = jnp.where(c == 0, um_ref[...], ux_ref[0])
    rowi = lax.broadcasted_iota(jnp.int32, u.shape, 0)
    prev = jnp.where(rowi == 0, carry_ref[...], pltpu.roll(u, 1, axis=0))
    carry_ref[...] = u[R - 1:R, :]
    us = u + mu_ref[...] * (prev - u)

    rr = us[:, 0:width]
    rk = us[:, width:2 * width]
    rv = us[:, 2 * width:3 * width]
    xwa = us[:, 3 * width:3 * width + LANES]
    xg = us[:, 3 * width + LANES:3 * width + 2 * LANES]

    lane_wa = lax.broadcasted_iota(jnp.int32, xwa.shape, 1)
    twa = jnp.where(lane_wa < HEAD_DIM, jnp.tanh(xwa), xwa)
    lo = _mm(twa, wlo_ref[...], pa=P_LORA, pb=P_LORA)
    dec_pre = dbase_ref[...] + lo[:, 0:width]
    ld = -jnp.exp(-_softplus(-dec_pre) - 0.5)
    a = jax.nn.sigmoid(abase_ref[...] + lo[:, width:2 * width])
    g = _mm(jax.nn.sigmoid(xg), gup_ref[...], pa=P_LORA, pb=P_LORA)

    bd = bd_ref[...]
    kkr = rk * kk_ref[...]
    ss = _mm(kkr * kkr, bd, pa=P_EXACT)
    kk = kkr / jnp.maximum(jnp.sqrt(ss), KK_EPS)
    k = rk * (1.0 + (a - 1.0) * ka_ref[...])
    beta = kk * a

    cs = _mm(tril_ref[...], ld, pb=P_EXACT)
    cs_tot = jnp.concatenate(
        [jnp.broadcast_to(cs[(j + 1) * C - 1:(j + 1) * C, :], (C, width)) for j in range(n_chunks)], axis=0)
    d_last = jnp.exp(cs_tot)
    e_neg = jnp.exp(-cs)
    e_last = jnp.exp(cs_tot - cs)
    a_t = (-kk * jnp.exp(cs - ld)).astype(BF16)
    r_t = rr * jnp.exp(cs)
    b_t = (beta * e_neg).astype(BF16)
    k_t = (k * e_neg).astype(BF16)
    b_h = (beta * e_last).astype(BF16)
    k_h = (k * e_last).astype(BF16)
    v_b = rv.astype(BF16)

    ri = lax.broadcasted_iota(jnp.int32, (R, R), 0)
    ci = lax.broadcasted_iota(jnp.int32, (R, R), 1)
    same = (ri // C) == (ci // C)
    strict = same & (ci < ri)
    incl = same & (ci <= ri)
    eye = ci == ri
    head0 = (lax.broadcasted_iota(jnp.int32, (R, LANES), 1) < HEAD_DIM)
    head0_2 = jnp.concatenate([head0, head0], axis=1)
    si = lax.broadcasted_iota(jnp.int32, (LANES, LANES), 0)
    sj = lax.broadcasted_iota(jnp.int32, (LANES, LANES), 1)
    st_same = (si // HEAD_DIM) == (sj // HEAD_DIM)
    st_eye = si == sj
    zero_b = jnp.zeros((R, LANES), BF16)

    for p in range(pairs):
        sl = slice(p * LANES, (p + 1) * LANES)
        A2, B2, K2, V2, Bh2, Kh2 = a_t[:, sl], b_t[:, sl], k_t[:, sl], v_b[:, sl], b_h[:, sl], k_h[:, sl]
        R2 = r_t[:, sl]
        R2b = R2.astype(BF16)
        t_inv, p_rb, p_rk, lv = [], [], [], []
        for h in range(2):
            hm = head0 if h == 0 else ~head0
            Ah = jnp.where(hm, A2, zero_b)
            Rh = jnp.where(hm, R2b, zero_b)
            l_ab = jnp.where(strict, _bdot(Ah, B2, NT), 0.0)
            l_ak = jnp.where(strict, _bdot(Ah, K2, NT), 0.0)
            p_rb.append(jnp.where(incl, _bdot(Rh, B2, NT), 0.0).astype(BF16))
            p_rk.append(jnp.where(incl, _bdot(Rh, K2, NT), 0.0).astype(BF16))
            ti = jnp.where(eye, 1.0, l_ab)
            pw = l_ab
            for _ in range(5):
                pwb = pw.astype(BF16)
                pw = _bdot(pwb, pwb)
                ti = ti + _bdot(ti, pw)
            t_inv.append(ti.astype(BF16))
            lv.append(_bdot(l_ak, V2))
        lv2 = jnp.where(head0, lv[0], lv[1]).astype(BF16)
        rhs = jnp.concatenate([A2, lv2], axis=1)
        wu = jnp.where(head0_2, _bdot(t_inv[0], rhs), _bdot(t_inv[1], rhs)).astype(BF16)
        pwu = jnp.where(head0_2, _bdot(p_rb[0], wu), _bdot(p_rb[1], wu))
        pv = jnp.where(head0, _bdot(p_rk[0], V2), _bdot(p_rk[1], V2))
        ya = (R2 + pwu[:, :LANES]).astype(BF16)
        yb = pwu[:, LANES:] + pv
        m = st_ref[p]
        for j in range(n_chunks):
            rs = slice(j * C, (j + 1) * C)
            gh = _bdot(Bh2[rs], wu[rs], TN)
            kv = _bdot(Kh2[rs], V2[rs], TN)
            dl = jnp.broadcast_to(d_last[j * C:j * C + 1, sl], (LANES, LANES))
            gm = jnp.where(st_same, gh[:, :LANES], 0.0) + jnp.where(st_eye, dl, 0.0)
            hm_ = jnp.where(st_same, gh[:, LANES:] + kv, 0.0)
            y_ref[rs, sl] = _bdot(ya[rs], m) + yb[rs]
            m = _bdot(gm, m) + hm_
        st_ref[p] = m

    y = y_ref[...]
    inv_n = 1.0 / HEAD_DIM
    mean = _mm(y, bd, pa=P_EXACT) * inv_n
    yc = y - mean
    var = _mm(yc * yc, bd, pa=P_EXACT) * inv_n
    yn = yc * lax.rsqrt(var + GN_EPS) * lng_ref[...] + lnb_ref[...]
    bonus = _mm(rr * k * rk_ref[...], bd, pa=P_EXACT) * rv
    o_ref[0] = ((yn + bonus) * g).astype(o_ref.dtype)


def _rwkv(u_x, u_meta, mu, w_lora, dbase, abase, gate_up, k_k, k_a, r_k, ln_g, ln_b):
    b, s, uw = u_x.shape
    width = dbase.shape[1]
    assert s % RW_ROWS == 0 and uw == 3 * width + 2 * LANES and width % LANES == 0
    nc = s // RW_ROWS + 1
    hid = jnp.arange(width) // HEAD_DIM
    bd = (hid[:, None] == hid[None, :]).astype(BF16)
    t = jnp.arange(RW_ROWS)
    tril = ((t[:, None] // CHUNK == t[None, :] // CHUNK) & (t[:, None] >= t[None, :])).astype(BF16)
    const = lambda shape: pl.BlockSpec(shape, lambda bi, c: tuple(0 for _ in shape))
    return pl.pallas_call(
        functools.partial(_rwkv_kernel, width=width),
        out_shape=jax.ShapeDtypeStruct((b, s, width), BF16),
        grid=(b, nc),
        in_specs=[
            pl.BlockSpec((1, RW_ROWS, uw), lambda bi, c: (bi, jnp.maximum(c - 1, 0), 0)),
            const((RW_ROWS, uw)), const((1, uw)), const((LANES, 2 * width)), const((1, width)),
            const((1, width)), const((LANES, width)), const((1, width)), const((1, width)),
            const((1, width)), const((1, width)), const((1, width)), const((width, width)),
            const((RW_ROWS, RW_ROWS)),
        ],
        out_specs=pl.BlockSpec((1, RW_ROWS, width), lambda bi, c: (bi, jnp.maximum(c - 1, 0), 0)),
        scratch_shapes=[
            pltpu.VMEM((width // LANES, LANES, LANES), F32),
            pltpu.VMEM((1, uw), F32),
            pltpu.VMEM((RW_ROWS, width), F32),
        ],
        compiler_params=pltpu.CompilerParams(
            dimension_semantics=("parallel", "arbitrary"), vmem_limit_bytes=VMEM_LIMIT),
        name="rwkv",
    )(u_x, u_meta, mu, w_lora, dbase, abase, gate_up, k_k, k_a, r_k, ln_g, ln_b, bd, tril)


def _merge_kernel(x_ref, osb_ref, orw_ref, gsb_ref, grw_ref, wsb_ref, wrw_ref, wout_ref, g_ref, o_ref):
    a = jnp.dot(osb_ref[...], wsb_ref[...], preferred_element_type=F32)
    b = jnp.dot(orw_ref[...], wrw_ref[...], preferred_element_type=F32)
    m = gsb_ref[...].astype(F32) * a + grw_ref[...].astype(F32) * b
    y = jnp.dot(m.astype(BF16), wout_ref[...], preferred_element_type=F32)
    o_ref[...] = x_ref[...] + _rms(y, g_ref[...])


def _merge(x2d, o_sb, o_rw, gates, w_sb, w_rw, w_out, gain, tm):
    m, d = x2d.shape
    width = o_sb.shape[1]
    assert m % tm == 0
    row = lambda cols, jb=0: pl.BlockSpec((tm, cols), lambda i: (i, jb))
    full = lambda shape: pl.BlockSpec(shape, lambda i: (0, 0))
    return pl.pallas_call(
        _merge_kernel,
        out_shape=jax.ShapeDtypeStruct((m, d), F32),
        grid=(m // tm,),
        in_specs=[row(d), row(width), row(width), row(d, 0), row(d, 1),
                  full((width, d)), full((width, d)), full((d, d)), full((1, d))],
        out_specs=row(d),
        compiler_params=pltpu.CompilerParams(
            dimension_semantics=("parallel",), vmem_limit_bytes=VMEM_LIMIT),
        name="merge",
    )(x2d, o_sb, o_rw, gates, gates, w_sb, w_rw, w_out, gain)


def _ffn_kernel(h_ref, gpre_ref, wg_ref, wu_ref, wd_ref, gpost_ref, o_ref, xn_ref, acc_ref):
    j = pl.program_id(1)

    @pl.when(j == 0)
    def _():
        xn_ref[...] = _rms(h_ref[...], gpre_ref[...]).astype(BF16)
        acc_ref[...] = jnp.zeros_like(acc_ref)

    xn = xn_ref[...]
    gate = jnp.dot(xn, wg_ref[...], preferred_element_type=F32)
    up = jnp.dot(xn, wu_ref[...], preferred_element_type=F32)
    act = (gate * jax.nn.sigmoid(gate) * up).astype(BF16)
    acc_ref[...] += jnp.dot(act, wd_ref[...], preferred_element_type=F32)

    @pl.when(j == pl.num_programs(1) - 1)
    def _():
        o_ref[...] = h_ref[...] + _rms(acc_ref[...], gpost_ref[...])


def _ffn(h2d, g_pre, w_gate, w_up, w_down, g_post, tm, tf):
    m, d = h2d.shape
    ff = w_gate.shape[1]
    assert m % tm == 0 and ff % tf == 0
    return pl.pallas_call(
        _ffn_kernel,
        out_shape=jax.ShapeDtypeStruct((m, d), F32),
        grid=(m // tm, ff // tf),
        in_specs=[
            pl.BlockSpec((tm, d), lambda i, j: (i, 0)),
            pl.BlockSpec((1, d), lambda i, j: (0, 0)),
            pl.BlockSpec((d, tf), lambda i, j: (0, j)),
            pl.BlockSpec((d, tf), lambda i, j: (0, j)),
            pl.BlockSpec((tf, d), lambda i, j: (j, 0)),
            pl.BlockSpec((1, d), lambda i, j: (0, 0)),
        ],
        out_specs=pl.BlockSpec((tm, d), lambda i, j: (i, 0)),
        scratch_shapes=[pltpu.VMEM((tm, d), BF16), pltpu.VMEM((tm, d), F32)],
        compiler_params=pltpu.CompilerParams(
            dimension_semantics=("parallel", "arbitrary"), vmem_limit_bytes=VMEM_LIMIT),
        name="ffn",
    )(h2d, g_pre, w_gate, w_up, w_down, g_post)


def _pick(n, prefs):
    for t in prefs:
        if n % t == 0:
            return t
    return n


def kernel(x, meta_tokens, norm_mix_pre, norm_mix_post, w_in, rw_shift_mu, rw_decay_up, rw_decay_base,
           rw_aaa_up, rw_aaa_base, rw_gate_up, rw_k_k, rw_k_a, rw_r_k, rw_ln_gain, rw_ln_bias,
           w_branch_sb, w_branch_rw, w_out, norm_ffn_pre, norm_ffn_post, w_ffn_gate, w_ffn_up,
           w_ffn_down):
    b, s, d = x.shape
    n_meta = meta_tokens.shape[0]
    depth = w_in.shape[0]
    assert depth == 1, "meta rows are only carried through the mixer of a single layer"
    width = w_branch_sb.shape[1]
    dlora = rw_decay_up.shape[1]
    alora = rw_aaa_up.shape[1]
    glora = rw_gate_up.shape[1]
    assert dlora + alora == LANES and glora == LANES and n_meta <= RW_ROWS
    l = 0
    m = b * s
    x2d = x.reshape(m, d)

    c_sb, c_rw = 3 * width, 3 * width + 3 * width + 2 * LANES
    w_in_b = w_in[l].astype(BF16)
    w_qkv, w_rw, w_gates = w_in_b[:, :c_sb], w_in_b[:, c_sb:c_rw], w_in_b[:, c_rw:]
    g_pre = norm_mix_pre[l][None, :]

    tm = _pick(m, (1024, 512, 256, 128))
    qkv = _proj(x2d, g_pre, w_qkv, BF16, None, tm, _pick(c_sb, (768, 512, 256, 128)), "proj_qkv")
    u_rw = _proj(x2d, g_pre, w_rw, F32, None, tm, _pick(c_rw - c_sb, (896, 512, 256, 128)), "proj_rw")
    gates = _proj(x2d, g_pre, w_gates, BF16, "sigmoid", tm, _pick(2 * d, (1024, 512, 256, 128)),
                  "proj_gates")
    meta_pad = jnp.zeros((LANES, d), F32).at[:n_meta].set(meta_tokens.astype(F32))
    qkv_meta = _proj(meta_pad, g_pre, w_qkv, BF16, None, LANES, _pick(c_sb, (768, 512, 256, 128)),
                     "proj_qkv_meta")
    rw_meta = _proj(meta_pad, g_pre, w_rw, F32, None, LANES, _pick(c_rw - c_sb, (896, 512, 256, 128)),
                    "proj_rw_meta")
    row_ok = (jnp.arange(LANES) < n_meta)[:, None]
    qkv_meta = jnp.where(row_ok, qkv_meta, jnp.zeros_like(qkv_meta))
    u_meta = jnp.zeros((RW_ROWS, c_rw - c_sb), F32).at[RW_ROWS - n_meta:].set(rw_meta[:n_meta])

    tq = _pick(s, (512, 256, 128))
    o_sb = _sb_attn(qkv.reshape(b, s, c_sb), qkv_meta, n_meta, tq, min(tq, 256))

    w_lora = jnp.zeros((LANES, 2 * width), F32)
    w_lora = w_lora.at[:dlora, :width].set(rw_decay_up[l]).at[dlora:, width:].set(rw_aaa_up[l])
    vec = lambda p: p[l].reshape(1, -1).astype(F32)
    o_rw = _rwkv(u_rw.reshape(b, s, c_rw - c_sb), u_meta, vec(rw_shift_mu), w_lora,
                 vec(rw_decay_base), vec(rw_aaa_base), rw_gate_up[l].astype(F32), vec(rw_k_k),
                 vec(rw_k_a), vec(rw_r_k), vec(rw_ln_gain), vec(rw_ln_bias))

    h1 = _merge(x2d, o_sb.reshape(m, width), o_rw.reshape(m, width), gates,
                w_branch_sb[l].astype(BF16), w_branch_rw[l].astype(BF16), w_out[l].astype(BF16),
                norm_mix_post[l][None, :], _pick(m, (512, 256, 128)))
    ff = w_ffn_gate.shape[2]
    out = _ffn(h1, norm_ffn_pre[l][None, :], w_ffn_gate[l].astype(BF16), w_ffn_up[l].astype(BF16),
               w_ffn_down[l].astype(BF16), norm_ffn_post[l][None, :], _pick(m, (512, 256, 128)),
               _pick(ff, (1408, 512, 256, 128)))
    return out.reshape(b, s, d)
```

```python
import functools

import jax
import jax.numpy as jnp
from jax import lax
from jax.experimental import pallas as pl
from jax.experimental.pallas import tpu as pltpu

HEAD_DIM = 64
RMS_EPS = 1e-6
GN_EPS = 64e-5
KK_EPS = 1e-12
LANES = 128
CHUNK = 64
RW_ROWS = 256
VMEM_LIMIT = 48 * 1024 * 1024
LOG2E = 1.4426950408889634
MASKED = 1e30

F32 = jnp.float32
BF16 = jnp.bfloat16

NN = (((1,), (0,)), ((), ()))
NT = (((1,), (1,)), ((), ()))
TN = (((0,), (0,)), ((), ()))


def _split(x, n):
    if x.dtype == BF16:
        return [x]
    parts = []
    r = x
    for i in range(n):
        p = r.astype(BF16)
        parts.append(p)
        if i + 1 < n:
            r = r - p.astype(F32)
    return parts


def _mm(a, b, dims=NN, pa=1, pb=1):
    a_parts = _split(a, pa)
    b_parts = _split(b, pb)
    order = max(len(a_parts), len(b_parts))
    acc = None
    for i, ai in enumerate(a_parts):
        for j, bj in enumerate(b_parts):
            if i + j >= order:
                continue
            t = lax.dot_general(ai, bj, dims, preferred_element_type=F32)
            acc = t if acc is None else acc + t
    return acc


def _bdot(a, b, dims=NN):
    return lax.dot_general(a.astype(BF16), b.astype(BF16), dims, preferred_element_type=F32)


def _softplus(z):
    return jnp.maximum(z, 0.0) + jnp.log1p(jnp.exp(-jnp.abs(z)))


def _rms(x, g):
    ms = jnp.mean(x * x, axis=-1, keepdims=True)
    return x * lax.rsqrt(ms + RMS_EPS) * g


def _proj_kernel(x_ref, g_ref, w_ref, o_ref, xn_ref, *, act):
    @pl.when(pl.program_id(1) == 0)
    def _():
        xn_ref[...] = _rms(x_ref[...], g_ref[...]).astype(BF16)

    y = jnp.dot(xn_ref[...], w_ref[...], preferred_element_type=F32)
    if act == "sigmoid":
        y = jax.nn.sigmoid(y)
    o_ref[...] = y.astype(o_ref.dtype)


def _proj(x2d, gain, w, out_dtype, act, tm, tn, name):
    m, d = x2d.shape
    n = w.shape[1]
    assert m % tm == 0 and n % tn == 0
    return pl.pallas_call(
        functools.partial(_proj_kernel, act=act),
        out_shape=jax.ShapeDtypeStruct((m, n), out_dtype),
        grid=(m // tm, n // tn),
        in_specs=[
            pl.BlockSpec((tm, d), lambda i, j: (i, 0)),
            pl.BlockSpec((1, d), lambda i, j: (0, 0)),
            pl.BlockSpec((d, tn), lambda i, j: (0, j)),
        ],
        out_specs=pl.BlockSpec((tm, tn), lambda i, j: (i, j)),
        scratch_shapes=[pltpu.VMEM((tm, d), BF16)],
        compiler_params=pltpu.CompilerParams(
            dimension_semantics=("parallel", "arbitrary"), vmem_limit_bytes=VMEM_LIMIT),
        name=name,
    )(x2d, gain, w)


def _sb_kernel(q_ref, k_ref, v_ref, km_ref, vm_ref, tri_ref, o_ref, acc_ref, *, tq, sub, n_meta):
    qi = pl.program_id(2)
    lane = lax.broadcasted_iota(jnp.int32, (tq, LANES), 1)
    q2 = (q_ref[0].astype(F32) * (HEAD_DIM ** -0.5 * LOG2E)).astype(BF16)
    zero = jnp.zeros_like(q2)
    qh = (jnp.where(lane < HEAD_DIM, q2, zero), jnp.where(lane >= HEAD_DIM, q2, zero))
    tri = tri_ref[...]

    def start(h, kblk, mask, tri_b):
        z = lax.dot_general(qh[h], kblk, NT, preferred_element_type=F32)
        if mask is not None:
            z = jnp.where(mask, z, -MASKED)
        neg_abs = lax.bitcast_convert_type(
            lax.bitcast_convert_type(z, jnp.uint32) | jnp.uint32(0x80000000), F32)
        sp = jnp.maximum(z, 0.0) + jnp.log2(1.0 + jnp.exp2(neg_abs))
        return z, jnp.dot(sp.astype(BF16), tri_b, preferred_element_type=F32)

    def finish(h, zc, carry, vblk):
        z, cs = zc
        c = cs + carry
        acc_ref[h] += jnp.dot(jnp.exp2(z - c).astype(BF16), vblk, preferred_element_type=F32)
        return c[:, 0:1]

    def block(k_of, v_of, carries, mask_of, n_sub):
        order = [(h, j) for j in reversed(range(n_sub)) for h in range(2)]
        carries = list(carries)
        started = {}
        for idx, (h, j) in enumerate(order):
            started[(h, j)] = start(h, k_of(j), mask_of(j), tri)
            if idx >= 1:
                ph, pj = order[idx - 1]
                carries[ph] = finish(ph, started.pop((ph, pj)), carries[ph], v_of(pj))
        ph, pj = order[-1]
        carries[ph] = finish(ph, started.pop((ph, pj)), carries[ph], v_of(pj))
        return tuple(carries)

    acc_ref[...] = jnp.zeros_like(acc_ref)
    row = lax.broadcasted_iota(jnp.int32, (tq, sub), 0)
    col = lax.broadcasted_iota(jnp.int32, (tq, sub), 1)
    n_sub = tq // sub
    k_at = lambda s0: (lambda j: k_ref[0, pl.ds(pl.multiple_of(s0 + j * sub, sub), sub), :])
    v_at = lambda s0: (lambda j: v_ref[0, pl.ds(pl.multiple_of(s0 + j * sub, sub), sub), :])
    d0 = qi * tq
    carries = block(k_at(d0), v_at(d0), (jnp.zeros((tq, 1), F32),) * 2,
                    lambda j: col + j * sub < row, n_sub)
    odd = qi % 2
    carries = lax.cond(
        odd == 1,
        lambda cr: block(k_at(d0 - tq), v_at(d0 - tq), cr, lambda j: None, n_sub),
        lambda cr: cr, carries)

    def body(i, carries):
        s0 = (qi - odd - 2 - 2 * i) * tq
        return block(k_at(s0), v_at(s0), carries, lambda j: None, 2 * n_sub)

    carries = lax.fori_loop(0, qi // 2, body, carries)
    meta_mask = lane < n_meta
    tri_m = tri_ref[0:LANES, 0:LANES]
    metas = [start(h, km_ref[...], meta_mask, tri_m) for h in range(2)]
    for h in range(2):
        finish(h, metas[h], carries[h], vm_ref[...])
    o_ref[0] = jnp.where(lane < HEAD_DIM, acc_ref[0], acc_ref[1]).astype(o_ref.dtype)


def _sb_attn(qkv, qkv_meta, n_meta, tq, sub):
    b, s, w3 = qkv.shape
    width = w3 // 3
    pairs = width // LANES
    assert s % tq == 0 and tq % sub == 0 and sub % LANES == 0
    tri = (jnp.arange(sub)[:, None] >= jnp.arange(sub)[None, :]).astype(BF16)
    return pl.pallas_call(
        functools.partial(_sb_kernel, tq=tq, sub=sub, n_meta=n_meta),
        out_shape=jax.ShapeDtypeStruct((b, s, width), BF16),
        grid=(b, pairs, s // tq),
        in_specs=[
            pl.BlockSpec((1, tq, LANES), lambda bi, p, qi: (bi, qi, p)),
            pl.BlockSpec((1, s, LANES), lambda bi, p, qi: (bi, 0, pairs + p)),
            pl.BlockSpec((1, s, LANES), lambda bi, p, qi: (bi, 0, 2 * pairs + p)),
            pl.BlockSpec((LANES, LANES), lambda bi, p, qi: (0, pairs + p)),
            pl.BlockSpec((LANES, LANES), lambda bi, p, qi: (0, 2 * pairs + p)),
            pl.BlockSpec((sub, sub), lambda bi, p, qi: (0, 0)),
        ],
        out_specs=pl.BlockSpec((1, tq, LANES), lambda bi, p, qi: (bi, qi, p)),
        scratch_shapes=[pltpu.VMEM((2, tq, LANES), F32)],
        compiler_params=pltpu.CompilerParams(
            dimension_semantics=("parallel", "parallel", "arbitrary"),
            vmem_limit_bytes=VMEM_LIMIT),
        name="sb_attn",
    )(qkv, qkv, qkv, qkv_meta, qkv_meta, tri)


P_LORA = 2
P_EXACT = 3
P_SUM = 2


def _rwkv_kernel(ux_ref, um_ref, mu_ref, wlo_ref, dbase_ref, abase_ref, gup_ref, kk_ref, ka_ref,
                 rk_ref, lng_ref, lnb_ref, bd_ref, tril_ref, o_ref, st_ref, carry_ref, y_ref,
                 *, width):
    c = pl.program_id(1)
    R, C = RW_ROWS, CHUNK
    n_chunks = R // C
    pairs = width // LANES

    @pl.when(c == 0)
    def _():
        st_ref[...] = jnp.zeros_like(st_ref)
        carry_ref[...] = jnp.zeros_like(carry_ref)

    u = jnp.where(c == 0, um_ref[...], ux_ref[0])
    rowi = lax.broadcasted_iota(jnp.int32, u.shape, 0)
    prev = jnp.where(rowi == 0, carry_ref[...], pltpu.roll(u, 1, axis=0))
    carry_ref[...] = u[R - 1:R, :]
    us = u + mu_ref[...] * (prev - u)

    rr = us[:, 0:width]
    rk = us[:, width:2 * width]
    rv = us[:, 2 * width:3 * width]
    xwa = us[:, 3 * width:3 * width + LANES]
    xg = us[:, 3 * width + LANES:3 * width + 2 * LANES]

    lane_wa = lax.broadcasted_iota(jnp.int32, xwa.shape, 1)
    twa = jnp.where(lane_wa < HEAD_DIM, jnp.tanh(xwa), xwa)
    lo = _mm(twa, wlo_ref[...], pa=P_LORA, pb=P_LORA)
    dec_pre = dbase_ref[...] + lo[:, 0:width]
    ld = -jnp.exp(-_softplus(-dec_pre) - 0.5)
    a = jax.nn.sigmoid(abase_ref[...] + lo[:, width:2 * width])
    g = _mm(jax.nn.sigmoid(xg), gup_ref[...], pa=P_LORA, pb=P_LORA)

    bd = bd_ref[...]
    kkr = rk * kk_ref[...]
    ss = _mm(kkr * kkr, bd, pa=P_SUM)
    kk = kkr / jnp.maximum(jnp.sqrt(ss), KK_EPS)
    k = rk * (1.0 + (a - 1.0) * ka_ref[...])
    beta = kk * a

    cs = _mm(tril_ref[...], ld, pb=P_EXACT)
    cs_tot = jnp.concatenate(
        [jnp.broadcast_to(cs[(j + 1) * C - 1:(j + 1) * C, :], (C, width)) for j in range(n_chunks)], axis=0)
    d_last = jnp.exp(cs_tot)
    e_neg = jnp.exp(-cs)
    e_last = jnp.exp(cs_tot - cs)
    a_t = (-kk * jnp.exp(cs - ld)).astype(BF16)
    r_t = rr * jnp.exp(cs)
    b_t = (beta * e_neg).astype(BF16)
    k_t = (k * e_neg).astype(BF16)
    b_h = (beta * e_last).astype(BF16)
    k_h = (k * e_last).astype(BF16)
    v_b = rv.astype(BF16)

    ri = lax.broadcasted_iota(jnp.int32, (R, R), 0)
    ci = lax.broadcasted_iota(jnp.int32, (R, R), 1)
    same = (ri // C) == (ci // C)
    strict = same & (ci < ri)
    incl = same & (ci <= ri)
    eye = ci == ri
    head0 = (lax.broadcasted_iota(jnp.int32, (R, LANES), 1) < HEAD_DIM)
    head0_2 = jnp.concatenate([head0, head0], axis=1)
    si = lax.broadcasted_iota(jnp.int32, (LANES, LANES), 0)
    sj = lax.broadcasted_iota(jnp.int32, (LANES, LANES), 1)
    st_same = (si // HEAD_DIM) == (sj // HEAD_DIM)
    st_eye = si == sj
    zero_b = jnp.zeros((R, LANES), BF16)

    heads = [(p, h) for p in range(pairs) for h in range(2)]
    sls = [slice(p * LANES, (p + 1) * LANES) for p in range(pairs)]
    hmask = (head0, ~head0)
    r_b = r_t.astype(BF16)

    def packed(x_bd):
        out = x_bd[0:C]
        for j in range(1, n_chunks):
            out = out + x_bd[j * C:(j + 1) * C]
        return out

    def block_diag(x_p):
        return jnp.where(same, jnp.concatenate([x_p.astype(BF16)] * n_chunks, axis=0), jnp.zeros((R, R), BF16))

    l_ab, l_ak, p_rb, p_rk = {}, {}, {}, {}
    for (p, h) in heads:
        Ah = jnp.where(hmask[h], a_t[:, sls[p]], zero_b)
        Rh = jnp.where(hmask[h], r_b[:, sls[p]], zero_b)
        l_ab[p, h] = packed(jnp.where(strict, _bdot(Ah, b_t[:, sls[p]], NT), 0.0))
        l_ak[p, h] = jnp.where(strict, _bdot(Ah, k_t[:, sls[p]], NT), 0.0).astype(BF16)
        p_rb[p, h] = jnp.where(incl, _bdot(Rh, b_t[:, sls[p]], NT), 0.0).astype(BF16)
        p_rk[p, h] = jnp.where(incl, _bdot(Rh, k_t[:, sls[p]], NT), 0.0).astype(BF16)

    eye_p = packed(jnp.where(eye, 1.0, 0.0))
    ti = {ph: eye_p + l_ab[ph] for ph in heads}
    pw = dict(l_ab)
    pw_bd = {ph: block_diag(pw[ph]) for ph in heads}
    for _ in range(5):
        for ph in heads:
            pw[ph] = _bdot(pw[ph], pw_bd[ph])
            pw_bd[ph] = block_diag(pw[ph])
            ti[ph] = ti[ph] + _bdot(ti[ph], pw_bd[ph])
    t_inv = {ph: block_diag(ti[ph]) for ph in heads}

    lv = {ph: _bdot(l_ak[ph], v_b[:, sls[ph[0]]]) for ph in heads}
    wu, ya, yb = {}, {}, {}
    for p in range(pairs):
        lv2 = jnp.where(head0, lv[p, 0], lv[p, 1]).astype(BF16)
        rhs = jnp.concatenate([a_t[:, sls[p]], lv2], axis=1)
        wu[p] = jnp.where(head0_2, _bdot(t_inv[p, 0], rhs), _bdot(t_inv[p, 1], rhs)).astype(BF16)
    for p in range(pairs):
        pwu = jnp.where(head0_2, _bdot(p_rb[p, 0], wu[p]), _bdot(p_rb[p, 1], wu[p]))
        pv = jnp.where(head0, _bdot(p_rk[p, 0], v_b[:, sls[p]]), _bdot(p_rk[p, 1], v_b[:, sls[p]]))
        ya[p] = (r_t[:, sls[p]] + pwu[:, :LANES]).astype(BF16)
        yb[p] = pwu[:, LANES:] + pv

    gm, hm_ = {}, {}
    for j in range(n_chunks):
        rs = slice(j * C, (j + 1) * C)
        for p in range(pairs):
            gh = _bdot(b_h[rs, sls[p]], wu[p][rs], TN)
            kv = _bdot(k_h[rs, sls[p]], v_b[rs, sls[p]], TN)
            dl = jnp.broadcast_to(d_last[j * C:j * C + 1, sls[p]], (LANES, LANES))
            gm[p, j] = (jnp.where(st_same, gh[:, :LANES], 0.0) + jnp.where(st_eye, dl, 0.0)).astype(BF16)
            hm_[p, j] = jnp.where(st_same, gh[:, LANES:] + kv, 0.0)

    m = {p: st_ref[p] for p in range(pairs)}
    for j in range(n_chunks):
        rs = slice(j * C, (j + 1) * C)
        for p in range(pairs):
            mb = m[p].astype(BF16)
            y_ref[rs, sls[p]] = _bdot(ya[p][rs], mb) + yb[p][rs]
            m[p] = _bdot(gm[p, j], mb) + hm_[p, j]
    for p in range(pairs):
        st_ref[p] = m[p]

    y = y_ref[...]
    inv_n = 1.0 / HEAD_DIM
    mean = _mm(y, bd, pa=P_SUM) * inv_n
    yc = y - mean
    var = _mm(yc * yc, bd, pa=P_SUM) * inv_n
    yn = yc * lax.rsqrt(var + GN_EPS) * lng_ref[...] + lnb_ref[...]
    bonus = _mm(rr * k * rk_ref[...], bd, pa=1) * rv
    o_ref[0] = ((yn + bonus) * g).astype(o_ref.dtype)


def _rwkv(u_x, u_meta, mu, w_lora, dbase, abase, gate_up, k_k, k_a, r_k, ln_g, ln_b):
    b, s, uw = u_x.shape
    width = dbase.shape[1]
    assert s % RW_ROWS == 0 and uw == 3 * width + 2 * LANES and width % LANES == 0
    nc = s // RW_ROWS + 1
    hid = jnp.arange(width) // HEAD_DIM
    bd = (hid[:, None] == hid[None, :]).astype(BF16)
    t = jnp.arange(RW_ROWS)
    tril = ((t[:, None] // CHUNK == t[None, :] // CHUNK) & (t[:, None] >= t[None, :])).astype(BF16)
    const = lambda shape: pl.BlockSpec(shape, lambda bi, c: tuple(0 for _ in shape))
    return pl.pallas_call(
        functools.partial(_rwkv_kernel, width=width),
        out_shape=jax.ShapeDtypeStruct((b, s, width), BF16),
        grid=(b, nc),
        in_specs=[
            pl.BlockSpec((1, RW_ROWS, uw), lambda bi, c: (bi, jnp.maximum(c - 1, 0), 0)),
            const((RW_ROWS, uw)), const((1, uw)), const((LANES, 2 * width)), const((1, width)),
            const((1, width)), const((LANES, width)), const((1, width)), const((1, width)),
            const((1, width)), const((1, width)), const((1, width)), const((width, width)),
            const((RW_ROWS, RW_ROWS)),
        ],
        out_specs=pl.BlockSpec((1, RW_ROWS, width), lambda bi, c: (bi, jnp.maximum(c - 1, 0), 0)),
        scratch_shapes=[
            pltpu.VMEM((width // LANES, LANES, LANES), F32),
            pltpu.VMEM((1, uw), F32),
            pltpu.VMEM((RW_ROWS, width), F32),
        ],
        compiler_params=pltpu.CompilerParams(
            dimension_semantics=("parallel", "arbitrary"), vmem_limit_bytes=VMEM_LIMIT),
        name="rwkv",
    )(u_x, u_meta, mu, w_lora, dbase, abase, gate_up, k_k, k_a, r_k, ln_g, ln_b, bd, tril)


def _merge_kernel(x_ref, osb_ref, orw_ref, gsb_ref, grw_ref, wsb_ref, wrw_ref, wout_ref, g_ref, o_ref):
    a = jnp.dot(osb_ref[...], wsb_ref[...], preferred_element_type=F32)
    b = jnp.dot(orw_ref[...], wrw_ref[...], preferred_element_type=F32)
    m = gsb_ref[...].astype(F32) * a + grw_ref[...].astype(F32) * b
    y = jnp.dot(m.astype(BF16), wout_ref[...], preferred_element_type=F32)
    o_ref[...] = x_ref[...] + _rms(y, g_ref[...])


def _merge(x2d, o_sb, o_rw, gates, w_sb, w_rw, w_out, gain, tm):
    m, d = x2d.shape
    width = o_sb.shape[1]
    assert m % tm == 0
    row = lambda cols, jb=0: pl.BlockSpec((tm, cols), lambda i: (i, jb))
    full = lambda shape: pl.BlockSpec(shape, lambda i: (0, 0))
    return pl.pallas_call(
        _merge_kernel,
        out_shape=jax.ShapeDtypeStruct((m, d), F32),
        grid=(m // tm,),
        in_specs=[row(d), row(width), row(width), row(d, 0), row(d, 1),
                  full((width, d)), full((width, d)), full((d, d)), full((1, d))],
        out_specs=row(d),
        compiler_params=pltpu.CompilerParams(
            dimension_semantics=("parallel",), vmem_limit_bytes=VMEM_LIMIT),
        name="merge",
    )(x2d, o_sb, o_rw, gates, gates, w_sb, w_rw, w_out, gain)


def _ffn_kernel(h_ref, gpre_ref, wg_ref, wu_ref, wd_ref, gpost_ref, o_ref, xn_ref, acc_ref):
    j = pl.program_id(1)

    @pl.when(j == 0)
    def _():
        xn_ref[...] = _rms(h_ref[...], gpre_ref[...]).astype(BF16)
        acc_ref[...] = jnp.zeros_like(acc_ref)

    xn = xn_ref[...]
    gate = jnp.dot(xn, wg_ref[...], preferred_element_type=F32)
    up = jnp.dot(xn, wu_ref[...], preferred_element_type=F32)
    act = (gate * jax.nn.sigmoid(gate) * up).astype(BF16)
    acc_ref[...] += jnp.dot(act, wd_ref[...], preferred_element_type=F32)

    @pl.when(j == pl.num_programs(1) - 1)
    def _():
        o_ref[...] = h_ref[...] + _rms(acc_ref[...], gpost_ref[...])


def _ffn(h2d, g_pre, w_gate, w_up, w_down, g_post, tm, tf):
    m, d = h2d.shape
    ff = w_gate.shape[1]
    assert m % tm == 0 and ff % tf == 0
    return pl.pallas_call(
        _ffn_kernel,
        out_shape=jax.ShapeDtypeStruct((m, d), F32),
        grid=(m // tm, ff // tf),
        in_specs=[
            pl.BlockSpec((tm, d), lambda i, j: (i, 0)),
            pl.BlockSpec((1, d), lambda i, j: (0, 0)),
            pl.BlockSpec((d, tf), lambda i, j: (0, j)),
            pl.BlockSpec((d, tf), lambda i, j: (0, j)),
            pl.BlockSpec((tf, d), lambda i, j: (j, 0)),
            pl.BlockSpec((1, d), lambda i, j: (0, 0)),
        ],
        out_specs=pl.BlockSpec((tm, d), lambda i, j: (i, 0)),
        scratch_shapes=[pltpu.VMEM((tm, d), BF16), pltpu.VMEM((tm, d), F32)],
        compiler_params=pltpu.CompilerParams(
            dimension_semantics=("parallel", "arbitrary"), vmem_limit_bytes=VMEM_LIMIT),
        name="ffn",
    )(h2d, g_pre, w_gate, w_up, w_down, g_post)


def _pick(n, prefs):
    for t in prefs:
        if n % t == 0:
            return t
    return n


def kernel(x, meta_tokens, norm_mix_pre, norm_mix_post, w_in, rw_shift_mu, rw_decay_up, rw_decay_base,
           rw_aaa_up, rw_aaa_base, rw_gate_up, rw_k_k, rw_k_a, rw_r_k, rw_ln_gain, rw_ln_bias,
           w_branch_sb, w_branch_rw, w_out, norm_ffn_pre, norm_ffn_post, w_ffn_gate, w_ffn_up,
           w_ffn_down):
    b, s, d = x.shape
    n_meta = meta_tokens.shape[0]
    depth = w_in.shape[0]
    assert depth == 1, "meta rows are only carried through the mixer of a single layer"
    width = w_branch_sb.shape[1]
    dlora = rw_decay_up.shape[1]
    alora = rw_aaa_up.shape[1]
    glora = rw_gate_up.shape[1]
    assert dlora + alora == LANES and glora == LANES and n_meta <= RW_ROWS
    l = 0
    m = b * s
    x2d = x.reshape(m, d)

    c_sb, c_rw = 3 * width, 3 * width + 3 * width + 2 * LANES
    w_in_b = w_in[l].astype(BF16)
    w_qkv, w_rw, w_gates = w_in_b[:, :c_sb], w_in_b[:, c_sb:c_rw], w_in_b[:, c_rw:]
    g_pre = norm_mix_pre[l][None, :]

    tm = _pick(m, (1024, 512, 256, 128))
    qkv = _proj(x2d, g_pre, w_qkv, BF16, None, tm, _pick(c_sb, (768, 512, 256, 128)), "proj_qkv")
    u_rw = _proj(x2d, g_pre, w_rw, F32, None, tm, _pick(c_rw - c_sb, (896, 512, 256, 128)), "proj_rw")
    gates = _proj(x2d, g_pre, w_gates, BF16, "sigmoid", tm, _pick(2 * d, (1024, 512, 256, 128)),
                  "proj_gates")
    meta_pad = jnp.zeros((LANES, d), F32).at[:n_meta].set(meta_tokens.astype(F32))
    qkv_meta = _proj(meta_pad, g_pre, w_qkv, BF16, None, LANES, _pick(c_sb, (768, 512, 256, 128)),
                     "proj_qkv_meta")
    rw_meta = _proj(meta_pad, g_pre, w_rw, F32, None, LANES, _pick(c_rw - c_sb, (896, 512, 256, 128)),
                    "proj_rw_meta")
    row_ok = (jnp.arange(LANES) < n_meta)[:, None]
    qkv_meta = jnp.where(row_ok, qkv_meta, jnp.zeros_like(qkv_meta))
    u_meta = jnp.zeros((RW_ROWS, c_rw - c_sb), F32).at[RW_ROWS - n_meta:].set(rw_meta[:n_meta])

    tq = _pick(s, (512, 256, 128))
    o_sb = _sb_attn(qkv.reshape(b, s, c_sb), qkv_meta, n_meta, tq, min(tq, 256))

    w_lora = jnp.zeros((LANES, 2 * width), F32)
    w_lora = w_lora.at[:dlora, :width].set(rw_decay_up[l]).at[dlora:, width:].set(rw_aaa_up[l])
    vec = lambda p: p[l].reshape(1, -1).astype(F32)
    o_rw = _rwkv(u_rw.reshape(b, s, c_rw - c_sb), u_meta, vec(rw_shift_mu), w_lora,
                 vec(rw_decay_base), vec(rw_aaa_base), rw_gate_up[l].astype(F32), vec(rw_k_k),
                 vec(rw_k_a), vec(rw_r_k), vec(rw_ln_gain), vec(rw_ln_bias))

    h1 = _merge(x2d, o_sb.reshape(m, width), o_rw.reshape(m, width), gates,
                w_branch_sb[l].astype(BF16), w_branch_rw[l].astype(BF16), w_out[l].astype(BF16),
                norm_mix_post[l][None, :], _pick(m, (512, 256, 128)))
    ff = w_ffn_gate.shape[2]
    out = _ffn(h1, norm_ffn_pre[l][None, :], w_ffn_gate[l].astype(BF16), w_ffn_up[l].astype(BF16),
               w_ffn_down[l].astype(BF16), norm_ffn_post[l][None, :], _pick(m, (512, 256, 128)),
               _pick(ff, (1408, 512, 256, 128)))
    return out.reshape(b, s, d)
```

```python
import functools

import jax
import jax.numpy as jnp
from jax import lax
from jax.experimental import pallas as pl
from jax.experimental.pallas import tpu as pltpu

HEAD_DIM = 64
RMS_EPS = 1e-6
GN_EPS = 64e-5
KK_EPS = 1e-12
LANES = 128
CHUNK = 64
RW_ROWS = 256
FF_CHUNK = 1024
VMEM_LIMIT = 56 * 1024 * 1024
LOG2E = 1.4426950408889634
MASKED = 1e30
CARRY_STOP = 64.0

F32 = jnp.float32
BF16 = jnp.bfloat16

NN = (((1,), (0,)), ((), ()))
NT = (((1,), (1,)), ((), ()))
TN = (((0,), (0,)), ((), ()))


def _split(x, n):
    if x.dtype == BF16:
        return [x]
    parts = []
    r = x
    for i in range(n):
        p = r.astype(BF16)
        parts.append(p)
        if i + 1 < n:
            r = r - p.astype(F32)
    return parts


def _mm(a, b, dims=NN, pa=1, pb=1):
    a_parts = _split(a, pa)
    b_parts = _split(b, pb)
    order = max(len(a_parts), len(b_parts))
    acc = None
    for i, ai in enumerate(a_parts):
        for j, bj in enumerate(b_parts):
            if i + j >= order:
                continue
            t = lax.dot_general(ai, bj, dims, preferred_element_type=F32)
            acc = t if acc is None else acc + t
    return acc


def _bdot(a, b, dims=NN):
    return lax.dot_general(a.astype(BF16), b.astype(BF16), dims, preferred_element_type=F32)


def _softplus(z):
    return jnp.maximum(z, 0.0) + jnp.log1p(jnp.exp(-jnp.abs(z)))


def _rms(x, g):
    ms = jnp.mean(x * x, axis=-1, keepdims=True)
    return x * lax.rsqrt(ms + RMS_EPS) * g


def _resident(shape):
    return pl.BlockSpec(shape, lambda i: (0,) * len(shape), pipeline_mode=pl.Buffered(1))


def _proj_kernel(x_ref, g_ref, w_ref, qkv_ref, rw_ref, gate_ref, *, c_sb, c_rw):
    xn = _rms(x_ref[...], g_ref[...]).astype(BF16)
    qkv_ref[...] = jnp.dot(xn, w_ref[:, 0:c_sb], preferred_element_type=F32).astype(qkv_ref.dtype)
    rw_ref[...] = jnp.dot(xn, w_ref[:, c_sb:c_rw], preferred_element_type=F32)
    gate_ref[...] = jax.nn.sigmoid(
        jnp.dot(xn, w_ref[:, c_rw:], preferred_element_type=F32)).astype(gate_ref.dtype)


def _proj(x2d, gain, w, c_sb, c_rw, tm, name):
    m, d = x2d.shape
    n = w.shape[1]
    assert m % tm == 0
    row = lambda cols: pl.BlockSpec((tm, cols), lambda i: (i, 0))
    return pl.pallas_call(
        functools.partial(_proj_kernel, c_sb=c_sb, c_rw=c_rw),
        out_shape=(jax.ShapeDtypeStruct((m, c_sb), BF16),
                   jax.ShapeDtypeStruct((m, c_rw - c_sb), F32),
                   jax.ShapeDtypeStruct((m, n - c_rw), BF16)),
        grid=(m // tm,),
        in_specs=[row(d), _resident((1, d)), _resident((d, n))],
        out_specs=(row(c_sb), row(c_rw - c_sb), row(n - c_rw)),
        compiler_params=pltpu.CompilerParams(
            dimension_semantics=("parallel",), vmem_limit_bytes=VMEM_LIMIT),
        name=name,
    )(x2d, gain, w)


def _sb_kernel(q_ref, k_ref, v_ref, km_ref, vm_ref, tri_ref, o_ref, acc_ref, *, tq, sub, n_meta):
    qi = pl.program_id(2)
    lane = lax.broadcasted_iota(jnp.int32, (tq, LANES), 1)
    q2 = (q_ref[0].astype(F32) * (HEAD_DIM ** -0.5 * LOG2E)).astype(BF16)
    zero = jnp.zeros_like(q2)
    qh = (jnp.where(lane < HEAD_DIM, q2, zero), jnp.where(lane >= HEAD_DIM, q2, zero))
    tri = tri_ref[...]

    def start(h, kblk, mask, tri_b):
        z = lax.dot_general(qh[h], kblk, NT, preferred_element_type=F32)
        if mask is not None:
            z = jnp.where(mask, z, -MASKED)
        sp = jnp.maximum(z, 0.0) + jnp.log2(1.0 + jnp.exp2(-jnp.abs(z)))
        return z, jnp.dot(sp.astype(BF16), tri_b, preferred_element_type=F32)

    def finish(h, zc, carry, vblk):
        z, cs = zc
        c = cs + carry
        acc_ref[h] += jnp.dot(jnp.exp2(z - c).astype(BF16), vblk, preferred_element_type=F32)
        return c[:, 0:1]

    def block(k_of, v_of, carries, mask_of, n_sub):
        order = [(h, j) for j in reversed(range(n_sub)) for h in range(2)]
        carries = list(carries)
        started = {}
        for idx, (h, j) in enumerate(order):
            started[(h, j)] = start(h, k_of(j), mask_of(j), tri)
            if idx >= 1:
                ph, pj = order[idx - 1]
                carries[ph] = finish(ph, started.pop((ph, pj)), carries[ph], v_of(pj))
        ph, pj = order[-1]
        carries[ph] = finish(ph, started.pop((ph, pj)), carries[ph], v_of(pj))
        return tuple(carries)

    acc_ref[...] = jnp.zeros_like(acc_ref)
    row = lax.broadcasted_iota(jnp.int32, (tq, sub), 0)
    col = lax.broadcasted_iota(jnp.int32, (tq, sub), 1)
    n_sub = tq // sub
    k_at = lambda s0: (lambda j: k_ref[0, pl.ds(pl.multiple_of(s0 + j * sub, sub), sub), :])
    v_at = lambda s0: (lambda j: v_ref[0, pl.ds(pl.multiple_of(s0 + j * sub, sub), sub), :])
    d0 = qi * tq
    carries = block(k_at(d0), v_at(d0), (jnp.zeros((tq, 1), F32),) * 2,
                    lambda j: col + j * sub < row, n_sub)
    def more(state):
        i, c0, c1 = state
        return jnp.logical_and(i < qi, jnp.min(jnp.minimum(c0, c1)) < CARRY_STOP)

    def body(state):
        i, c0, c1 = state
        s0 = (qi - 1 - i) * tq
        c0, c1 = block(k_at(s0), v_at(s0), (c0, c1), lambda j: None, n_sub)
        return i + 1, c0, c1

    _, c0, c1 = lax.while_loop(more, body, (jnp.int32(0),) + tuple(carries))
    carries = (c0, c1)
    meta_mask = lane < n_meta
    tri_m = tri_ref[0:LANES, 0:LANES]
    metas = [start(h, km_ref[...], meta_mask, tri_m) for h in range(2)]
    for h in range(2):
        finish(h, metas[h], carries[h], vm_ref[...])
    o_ref[0] = jnp.where(lane < HEAD_DIM, acc_ref[0], acc_ref[1]).astype(o_ref.dtype)


def _sb_attn(qkv, qkv_meta, n_meta, tq, sub):
    b, s, w3 = qkv.shape
    width = w3 // 3
    pairs = width // LANES
    assert s % tq == 0 and tq % sub == 0 and sub % LANES == 0
    tri = (jnp.arange(sub)[:, None] >= jnp.arange(sub)[None, :]).astype(BF16)
    return pl.pallas_call(
        functools.partial(_sb_kernel, tq=tq, sub=sub, n_meta=n_meta),
        out_shape=jax.ShapeDtypeStruct((b, s, width), BF16),
        grid=(b, pairs, s // tq),
        in_specs=[
            pl.BlockSpec((1, tq, LANES), lambda bi, p, qi: (bi, qi, p)),
            pl.BlockSpec((1, s, LANES), lambda bi, p, qi: (bi, 0, pairs + p)),
            pl.BlockSpec((1, s, LANES), lambda bi, p, qi: (bi, 0, 2 * pairs + p)),
            pl.BlockSpec((LANES, LANES), lambda bi, p, qi: (0, pairs + p)),
            pl.BlockSpec((LANES, LANES), lambda bi, p, qi: (0, 2 * pairs + p)),
            pl.BlockSpec((sub, sub), lambda bi, p, qi: (0, 0)),
        ],
        out_specs=pl.BlockSpec((1, tq, LANES), lambda bi, p, qi: (bi, qi, p)),
        scratch_shapes=[pltpu.VMEM((2, tq, LANES), F32)],
        compiler_params=pltpu.CompilerParams(
            dimension_semantics=("parallel", "parallel", "arbitrary"),
            vmem_limit_bytes=VMEM_LIMIT),
        name="sb_attn",
    )(qkv, qkv, qkv, qkv_meta, qkv_meta, tri)


P_LORA = 2
P_EXACT = 3
P_SUM = 2


def _rwkv_kernel(ux_ref, um_ref, mu_ref, wlo_ref, dbase_ref, abase_ref, gup_ref, kk_ref, ka_ref,
                 rk_ref, lng_ref, lnb_ref, bd_ref, tril_ref, o_ref, st_ref, carry_ref, y_ref,
                 *, width):
    c = pl.program_id(1)
    R, C = RW_ROWS, CHUNK
    n_chunks = R // C
    pairs = width // LANES

    @pl.when(c == 0)
    def _():
        st_ref[...] = jnp.zeros_like(st_ref)
        carry_ref[...] = jnp.zeros_like(carry_ref)

    u = jnp.where(c == 0, um_ref[...], ux_ref[0])
    rowi = lax.broadcasted_iota(jnp.int32, u.shape, 0)
    prev = jnp.where(rowi == 0, carry_ref[...], pltpu.roll(u, 1, axis=0))
    carry_ref[...] = u[R - 1:R, :]
    us = u + mu_ref[...] * (prev - u)

    rr = us[:, 0:width]
    rk = us[:, width:2 * width]
    rv = us[:, 2 * width:3 * width]
    xwa = us[:, 3 * width:3 * width + LANES]
    xg = us[:, 3 * width + LANES:3 * width + 2 * LANES]

    lane_wa = lax.broadcasted_iota(jnp.int32, xwa.shape, 1)
    twa = jnp.where(lane_wa < HEAD_DIM, jnp.tanh(xwa), xwa)
    lo = _mm(twa, wlo_ref[...], pa=P_LORA, pb=P_LORA)
    dec_pre = dbase_ref[...] + lo[:, 0:width]
    ld = -jnp.exp(-_softplus(-dec_pre) - 0.5)
    a = jax.nn.sigmoid(abase_ref[...] + lo[:, width:2 * width])
    g = _mm(jax.nn.sigmoid(xg), gup_ref[...], pa=P_LORA, pb=P_LORA)

    bd = bd_ref[...]
    kkr = rk * kk_ref[...]
    ss = _mm(kkr * kkr, bd, pa=P_SUM)
    kk = kkr / jnp.maximum(jnp.sqrt(ss), KK_EPS)
    k = rk * (1.0 + (a - 1.0) * ka_ref[...])
    beta = kk * a

    cs = _mm(tril_ref[...], ld, pb=P_EXACT)
    cs_tot = jnp.concatenate(
        [jnp.broadcast_to(cs[(j + 1) * C - 1:(j + 1) * C, :], (C, width)) for j in range(n_chunks)], axis=0)
    d_last = jnp.exp(cs_tot)
    e_neg = jnp.exp(-cs)
    e_last = jnp.exp(cs_tot - cs)
    a_t = (-kk * jnp.exp(cs - ld)).astype(BF16)
    r_t = rr * jnp.exp(cs)
    b_t = (beta * e_neg).astype(BF16)
    k_t = (k * e_neg).astype(BF16)
    b_h = (beta * e_last).astype(BF16)
    k_h = (k * e_last).astype(BF16)
    v_b = rv.astype(BF16)

    ri = lax.broadcasted_iota(jnp.int32, (R, R), 0)
    ci = lax.broadcasted_iota(jnp.int32, (R, R), 1)
    same = (ri // C) == (ci // C)
    strict = same & (ci < ri)
    incl = same & (ci <= ri)
    eye = ci == ri
    head0 = (lax.broadcasted_iota(jnp.int32, (R, LANES), 1) < HEAD_DIM)
    head0_2 = jnp.concatenate([head0, head0], axis=1)
    si = lax.broadcasted_iota(jnp.int32, (LANES, LANES), 0)
    sj = lax.broadcasted_iota(jnp.int32, (LANES, LANES), 1)
    st_same = (si // HEAD_DIM) == (sj // HEAD_DIM)
    st_eye = si == sj
    zero_b = jnp.zeros((R, LANES), BF16)

    heads = [(p, h) for p in range(pairs) for h in range(2)]
    sls = [slice(p * LANES, (p + 1) * LANES) for p in range(pairs)]
    hmask = (head0, ~head0)
    r_b = r_t.astype(BF16)

    def packed(x_bd):
        out = x_bd[0:C]
        for j in range(1, n_chunks):
            out = out + x_bd[j * C:(j + 1) * C]
        return out

    def block_diag(x_p):
        return jnp.where(same, jnp.concatenate([x_p.astype(BF16)] * n_chunks, axis=0), jnp.zeros((R, R), BF16))

    l_ab, l_ak, p_rb, p_rk = {}, {}, {}, {}
    for (p, h) in heads:
        Ah = jnp.where(hmask[h], a_t[:, sls[p]], zero_b)
        Rh = jnp.where(hmask[h], r_b[:, sls[p]], zero_b)
        l_ab[p, h] = packed(jnp.where(strict, _bdot(Ah, b_t[:, sls[p]], NT), 0.0))
        l_ak[p, h] = jnp.where(strict, _bdot(Ah, k_t[:, sls[p]], NT), 0.0).astype(BF16)
        p_rb[p, h] = jnp.where(incl, _bdot(Rh, b_t[:, sls[p]], NT), 0.0).astype(BF16)
        p_rk[p, h] = jnp.where(incl, _bdot(Rh, k_t[:, sls[p]], NT), 0.0).astype(BF16)

    eye_p = packed(jnp.where(eye, 1.0, 0.0))
    ti = {ph: eye_p + l_ab[ph] for ph in heads}
    pw = dict(l_ab)
    pw_bd = {ph: block_diag(pw[ph]) for ph in heads}
    for _ in range(5):
        for ph in heads:
            pw[ph] = _bdot(pw[ph], pw_bd[ph])
            pw_bd[ph] = block_diag(pw[ph])
            ti[ph] = ti[ph] + _bdot(ti[ph], pw_bd[ph])
    t_inv = {ph: block_diag(ti[ph]) for ph in heads}

    lv = {ph: _bdot(l_ak[ph], v_b[:, sls[ph[0]]]) for ph in heads}
    wu, ya, yb = {}, {}, {}
    for p in range(pairs):
        lv2 = jnp.where(head0, lv[p, 0], lv[p, 1]).astype(BF16)
        rhs = jnp.concatenate([a_t[:, sls[p]], lv2], axis=1)
        wu[p] = jnp.where(head0_2, _bdot(t_inv[p, 0], rhs), _bdot(t_inv[p, 1], rhs)).astype(BF16)
    for p in range(pairs):
        pwu = jnp.where(head0_2, _bdot(p_rb[p, 0], wu[p]), _bdot(p_rb[p, 1], wu[p]))
        pv = jnp.where(head0, _bdot(p_rk[p, 0], v_b[:, sls[p]]), _bdot(p_rk[p, 1], v_b[:, sls[p]]))
        ya[p] = (r_t[:, sls[p]] + pwu[:, :LANES]).astype(BF16)
        yb[p] = pwu[:, LANES:] + pv

    gm, hm_ = {}, {}
    for j in range(n_chunks):
        rs = slice(j * C, (j + 1) * C)
        for p in range(pairs):
            gh = _bdot(b_h[rs, sls[p]], wu[p][rs], TN)
            kv = _bdot(k_h[rs, sls[p]], v_b[rs, sls[p]], TN)
            dl = jnp.broadcast_to(d_last[j * C:j * C + 1, sls[p]], (LANES, LANES))
            gm[p, j] = (jnp.where(st_same, gh[:, :LANES], 0.0) + jnp.where(st_eye, dl, 0.0)).astype(BF16)
            hm_[p, j] = jnp.where(st_same, gh[:, LANES:] + kv, 0.0)

    m = {p: st_ref[p] for p in range(pairs)}
    for j in range(n_chunks):
        rs = slice(j * C, (j + 1) * C)
        for p in range(pairs):
            mb = m[p].astype(BF16)
            y_ref[rs, sls[p]] = _bdot(ya[p][rs], mb) + yb[p][rs]
            m[p] = _bdot(gm[p, j], mb) + hm_[p, j]
    for p in range(pairs):
        st_ref[p] = m[p]

    y = y_ref[...]
    inv_n = 1.0 / HEAD_DIM
    mean = _mm(y, bd, pa=P_SUM) * inv_n
    yc = y - mean
    var = _mm(yc * yc, bd, pa=P_SUM) * inv_n
    yn = yc * lax.rsqrt(var + GN_EPS) * lng_ref[...] + lnb_ref[...]
    bonus = _mm(rr * k * rk_ref[...], bd, pa=1) * rv
    o_ref[0] = ((yn + bonus) * g).astype(o_ref.dtype)


def _rwkv(u_x, u_meta, mu, w_lora, dbase, abase, gate_up, k_k, k_a, r_k, ln_g, ln_b):
    b, s, uw = u_x.shape
    width = dbase.shape[1]
    assert s % RW_ROWS == 0 and uw == 3 * width + 2 * LANES and width % LANES == 0
    nc = s // RW_ROWS + 1
    hid = jnp.arange(width) // HEAD_DIM
    bd = (hid[:, None] == hid[None, :]).astype(BF16)
    t = jnp.arange(RW_ROWS)
    tril = ((t[:, None] // CHUNK == t[None, :] // CHUNK) & (t[:, None] >= t[None, :])).astype(BF16)
    const = lambda shape: pl.BlockSpec(shape, lambda bi, c: tuple(0 for _ in shape))
    return pl.pallas_call(
        functools.partial(_rwkv_kernel, width=width),
        out_shape=jax.ShapeDtypeStruct((b, s, width), BF16),
        grid=(b, nc),
        in_specs=[
            pl.BlockSpec((1, RW_ROWS, uw), lambda bi, c: (bi, jnp.maximum(c - 1, 0), 0)),
            const((RW_ROWS, uw)), const((1, uw)), const((LANES, 2 * width)), const((1, width)),
            const((1, width)), const((LANES, width)), const((1, width)), const((1, width)),
            const((1, width)), const((1, width)), const((1, width)), const((width, width)),
            const((RW_ROWS, RW_ROWS)),
        ],
        out_specs=pl.BlockSpec((1, RW_ROWS, width), lambda bi, c: (bi, jnp.maximum(c - 1, 0), 0)),
        scratch_shapes=[
            pltpu.VMEM((width // LANES, LANES, LANES), F32),
            pltpu.VMEM((1, uw), F32),
            pltpu.VMEM((RW_ROWS, width), F32),
        ],
        compiler_params=pltpu.CompilerParams(
            dimension_semantics=("parallel", "arbitrary"), vmem_limit_bytes=VMEM_LIMIT),
        name="rwkv",
    )(u_x, u_meta, mu, w_lora, dbase, abase, gate_up, k_k, k_a, r_k, ln_g, ln_b, bd, tril)


def _merge_ffn_kernel(x_ref, osb_ref, orw_ref, gsb_ref, grw_ref, wsb_ref, wrw_ref, wout_ref, gmix_ref,
                      gpre_ref, wg_ref, wu_ref, wd_ref, gpost_ref, o_ref, *, ff_chunks):
    a = jnp.dot(osb_ref[...], wsb_ref[...], preferred_element_type=F32)
    b = jnp.dot(orw_ref[...], wrw_ref[...], preferred_element_type=F32)
    merged = gsb_ref[...].astype(F32) * a + grw_ref[...].astype(F32) * b
    y = jnp.dot(merged.astype(BF16), wout_ref[...], preferred_element_type=F32)
    h1 = x_ref[...] + _rms(y, gmix_ref[...])
    xn = _rms(h1, gpre_ref[...]).astype(BF16)
    f = None
    for c0, c1 in ff_chunks:
        gate = jnp.dot(xn, wg_ref[:, c0:c1], preferred_element_type=F32)
        up = jnp.dot(xn, wu_ref[:, c0:c1], preferred_element_type=F32)
        act = (gate * jax.nn.sigmoid(gate) * up).astype(BF16)
        t = jnp.dot(act, wd_ref[c0:c1, :], preferred_element_type=F32)
        f = t if f is None else f + t
    o_ref[...] = h1 + _rms(f, gpost_ref[...])


def _merge_ffn(x2d, o_sb, o_rw, gates, w_sb, w_rw, w_out, g_mix, g_pre, w_gate, w_up, w_down, g_post, tm):
    m, d = x2d.shape
    width = o_sb.shape[1]
    ff = w_gate.shape[1]
    assert m % tm == 0 and ff % LANES == 0
    edges = list(range(0, ff, FF_CHUNK)) + [ff]
    ff_chunks = tuple(zip(edges[:-1], edges[1:]))
    row = lambda cols, jb=0: pl.BlockSpec((tm, cols), lambda i: (i, jb))
    return pl.pallas_call(
        functools.partial(_merge_ffn_kernel, ff_chunks=ff_chunks),
        out_shape=jax.ShapeDtypeStruct((m, d), F32),
        grid=(m // tm,),
        in_specs=[row(d), row(width), row(width), row(d, 0), row(d, 1),
                  _resident((width, d)), _resident((width, d)), _resident((d, d)), _resident((1, d)),
                  _resident((1, d)), _resident((d, ff)), _resident((d, ff)), _resident((ff, d)),
                  _resident((1, d))],
        out_specs=row(d),
        compiler_params=pltpu.CompilerParams(
            dimension_semantics=("parallel",), vmem_limit_bytes=VMEM_LIMIT),
        name="merge_ffn",
    )(x2d, o_sb, o_rw, gates, gates, w_sb, w_rw, w_out, g_mix, g_pre, w_gate, w_up, w_down, g_post)


def _pick(n, prefs):
    for t in prefs:
        if n % t == 0:
            return t
    return n


def kernel(x, meta_tokens, norm_mix_pre, norm_mix_post, w_in, rw_shift_mu, rw_decay_up, rw_decay_base,
           rw_aaa_up, rw_aaa_base, rw_gate_up, rw_k_k, rw_k_a, rw_r_k, rw_ln_gain, rw_ln_bias,
           w_branch_sb, w_branch_rw, w_out, norm_ffn_pre, norm_ffn_post, w_ffn_gate, w_ffn_up,
           w_ffn_down):
    b, s, d = x.shape
    n_meta = meta_tokens.shape[0]
    depth = w_in.shape[0]
    assert depth == 1, "meta rows are only carried through the mixer of a single layer"
    width = w_branch_sb.shape[1]
    dlora = rw_decay_up.shape[1]
    alora = rw_aaa_up.shape[1]
    glora = rw_gate_up.shape[1]
    assert dlora + alora == LANES and glora == LANES and n_meta <= RW_ROWS
    l = 0
    m = b * s
    x2d = x.reshape(m, d)

    c_sb, c_rw = 3 * width, 3 * width + 3 * width + 2 * LANES
    w_in_b = w_in[l].astype(BF16)
    g_pre = norm_mix_pre[l][None, :]

    qkv, u_rw, gates = _proj(x2d, g_pre, w_in_b, c_sb, c_rw, _pick(m, (512, 256, 128)), "proj")
    meta_pad = jnp.zeros((LANES, d), F32).at[:n_meta].set(meta_tokens.astype(F32))
    qkv_meta, rw_meta, _ = _proj(meta_pad, g_pre, w_in_b, c_sb, c_rw, LANES, "proj_meta")
    row_ok = (jnp.arange(LANES) < n_meta)[:, None]
    qkv_meta = jnp.where(row_ok, qkv_meta, jnp.zeros_like(qkv_meta))
    u_meta = jnp.zeros((RW_ROWS, c_rw - c_sb), F32).at[RW_ROWS - n_meta:].set(rw_meta[:n_meta])

    tq = _pick(s, (512, 256, 128))
    o_sb = _sb_attn(qkv.reshape(b, s, c_sb), qkv_meta, n_meta, tq, min(tq, 256))

    w_lora = jnp.zeros((LANES, 2 * width), F32)
    w_lora = w_lora.at[:dlora, :width].set(rw_decay_up[l]).at[dlora:, width:].set(rw_aaa_up[l])
    vec = lambda p: p[l].reshape(1, -1).astype(F32)
    o_rw = _rwkv(u_rw.reshape(b, s, c_rw - c_sb), u_meta, vec(rw_shift_mu), w_lora,
                 vec(rw_decay_base), vec(rw_aaa_base), rw_gate_up[l].astype(F32), vec(rw_k_k),
                 vec(rw_k_a), vec(rw_r_k), vec(rw_ln_gain), vec(rw_ln_bias))

    out = _merge_ffn(x2d, o_sb.reshape(m, width), o_rw.reshape(m, width), gates,
                     w_branch_sb[l].astype(BF16), w_branch_rw[l].astype(BF16), w_out[l].astype(BF16),
                     norm_mix_post[l][None, :], norm_ffn_pre[l][None, :], w_ffn_gate[l].astype(BF16),
                     w_ffn_up[l].astype(BF16), w_ffn_down[l].astype(BF16), norm_ffn_post[l][None, :],
                     _pick(m, (512, 256, 128)))
    return out.reshape(b, s, d)
```

```python
import functools

import jax
import jax.numpy as jnp
from jax import lax
from jax.experimental import pallas as pl
from jax.experimental.pallas import tpu as pltpu

HEAD_DIM = 64
RMS_EPS = 1e-6
GN_EPS = 64e-5
KK_EPS = 1e-12
LANES = 128
CHUNK = 64
RW_ROWS = 256
FF_CHUNK = 1024
VMEM_LIMIT = 56 * 1024 * 1024
LOG2E = 1.4426950408889634
MASKED = 1e30
CARRY_STOP = 64.0

F32 = jnp.float32
BF16 = jnp.bfloat16

NN = (((1,), (0,)), ((), ()))
NT = (((1,), (1,)), ((), ()))
TN = (((0,), (0,)), ((), ()))


def _split(x, n):
    if x.dtype == BF16:
        return [x]
    parts = []
    r = x
    for i in range(n):
        p = r.astype(BF16)
        parts.append(p)
        if i + 1 < n:
            r = r - p.astype(F32)
    return parts


def _mm(a, b, dims=NN, pa=1, pb=1):
    a_parts = _split(a, pa)
    b_parts = _split(b, pb)
    order = max(len(a_parts), len(b_parts))
    acc = None
    for i, ai in enumerate(a_parts):
        for j, bj in enumerate(b_parts):
            if i + j >= order:
                continue
            t = lax.dot_general(ai, bj, dims, preferred_element_type=F32)
            acc = t if acc is None else acc + t
    return acc


def _bdot(a, b, dims=NN):
    return lax.dot_general(a.astype(BF16), b.astype(BF16), dims, preferred_element_type=F32)


def _softplus(z):
    return jnp.maximum(z, 0.0) + jnp.log1p(jnp.exp(-jnp.abs(z)))


def _rms(x, g):
    ms = jnp.mean(x * x, axis=-1, keepdims=True)
    return x * lax.rsqrt(ms + RMS_EPS) * g


def _resident(shape):
    return pl.BlockSpec(shape, lambda i: (0,) * len(shape), pipeline_mode=pl.Buffered(1))


def _proj_kernel(x_ref, g_ref, w_ref, qkv_ref, rw_ref, gate_ref, *, c_sb, c_rw):
    xn = _rms(x_ref[...], g_ref[...]).astype(BF16)
    qkv_ref[...] = jnp.dot(xn, w_ref[:, 0:c_sb], preferred_element_type=F32).astype(qkv_ref.dtype)
    rw_ref[...] = jnp.dot(xn, w_ref[:, c_sb:c_rw], preferred_element_type=F32)
    gate_ref[...] = jax.nn.sigmoid(
        jnp.dot(xn, w_ref[:, c_rw:], preferred_element_type=F32)).astype(gate_ref.dtype)


def _proj(x2d, gain, w, c_sb, c_rw, tm, name):
    m, d = x2d.shape
    n = w.shape[1]
    assert m % tm == 0
    row = lambda cols: pl.BlockSpec((tm, cols), lambda i: (i, 0))
    return pl.pallas_call(
        functools.partial(_proj_kernel, c_sb=c_sb, c_rw=c_rw),
        out_shape=(jax.ShapeDtypeStruct((m, c_sb), BF16),
                   jax.ShapeDtypeStruct((m, c_rw - c_sb), F32),
                   jax.ShapeDtypeStruct((m, n - c_rw), BF16)),
        grid=(m // tm,),
        in_specs=[row(d), _resident((1, d)), _resident((d, n))],
        out_specs=(row(c_sb), row(c_rw - c_sb), row(n - c_rw)),
        compiler_params=pltpu.CompilerParams(
            dimension_semantics=("parallel",), vmem_limit_bytes=VMEM_LIMIT),
        name=name,
    )(x2d, gain, w)


def _sb_kernel(q_ref, k_ref, v_ref, km_ref, vm_ref, tri_ref, o_ref, acc_ref, *, tq, sub, n_meta):
    qi = pl.program_id(2)
    lane = lax.broadcasted_iota(jnp.int32, (tq, LANES), 1)
    q2 = (q_ref[0].astype(F32) * (HEAD_DIM ** -0.5 * LOG2E)).astype(BF16)
    zero = jnp.zeros_like(q2)
    qh = (jnp.where(lane < HEAD_DIM, q2, zero), jnp.where(lane >= HEAD_DIM, q2, zero))
    tri = tri_ref[...]

    n_row = tq // sub
    assert n_row in (1, 2), "the block schedule below covers one or two row groups"
    rows = lambda a: slice(a * sub, (a + 1) * sub)

    def start(h, a, kblk, mask, tri_b):
        z = lax.dot_general(qh[h][rows(a)], kblk, NT, preferred_element_type=F32)
        if mask is not None:
            z = jnp.where(mask, z, -MASKED)
        sp = jnp.maximum(z, 0.0) + jnp.log2(1.0 + jnp.exp2(-jnp.abs(z)))
        return z, jnp.dot(sp.astype(BF16), tri_b, preferred_element_type=F32)

    def finish(h, a, zc, carry, vblk):
        z, cs = zc
        c = cs + carry
        acc_ref[h, rows(a), :] += jnp.dot(jnp.exp2(z - c).astype(BF16), vblk, preferred_element_type=F32)
        return c[:, 0:1]

    def run(tiles, carries):
        carries = dict(carries)
        pending = None
        for (h, a, kblk, vblk, mask, tri_b) in tiles:
            zc = start(h, a, kblk, mask, tri_b)
            if pending is not None:
                ph, pa, pzc, pv = pending
                carries[ph, pa] = finish(ph, pa, pzc, carries[ph, pa], pv)
            pending = (h, a, zc, vblk)
        ph, pa, pzc, pv = pending
        carries[ph, pa] = finish(ph, pa, pzc, carries[ph, pa], pv)
        return carries

    def min_carry(carries):
        m = None
        for c in carries.values():
            m = c if m is None else jnp.minimum(m, c)
        return jnp.min(m)

    acc_ref[...] = jnp.zeros_like(acc_ref)
    keys_at = lambda s0: k_ref[0, pl.ds(pl.multiple_of(s0, sub), sub), :]
    vals_at = lambda s0: v_ref[0, pl.ds(pl.multiple_of(s0, sub), sub), :]
    ri = lax.broadcasted_iota(jnp.int32, (sub, sub), 0)
    ci = lax.broadcasted_iota(jnp.int32, (sub, sub), 1)
    causal = ci < ri
    d0 = qi * tq
    heads_rows = [(h, a) for a in range(n_row) for h in range(2)]

    tiles = [(h, a, keys_at(d0 + a * sub), vals_at(d0 + a * sub), causal, tri) for (h, a) in heads_rows]
    for (h, a) in heads_rows:
        if a == 0:
            s0 = jnp.maximum(d0 - sub, 0)
            tiles.append((h, a, keys_at(s0), vals_at(s0), jnp.broadcast_to(qi > 0, (sub, sub)), tri))
        else:
            s0 = d0 + (a - 1) * sub
            tiles.append((h, a, keys_at(s0), vals_at(s0), None, tri))
    carries = run(tiles, {ha: jnp.zeros((sub, 1), F32) for ha in heads_rows})

    n_left = qi * n_row

    def more(state):
        return jnp.logical_and(state[0] < n_left, state[1] < CARRY_STOP)

    def body(state):
        i = state[0]
        carries = dict(zip(heads_rows, state[2:]))
        s0 = (n_left - 1 - i) * sub
        kblk, vblk = keys_at(s0), vals_at(s0)
        tiles = []
        for (h, a) in heads_rows:
            fresh = jnp.broadcast_to(i >= 1, (sub, sub)) if a == 0 else None
            tiles.append((h, a, kblk, vblk, fresh, tri))
        carries = run(tiles, carries)
        return (i + 1, min_carry(carries)) + tuple(carries[ha] for ha in heads_rows)

    state = (jnp.int32(0), min_carry(carries)) + tuple(carries[ha] for ha in heads_rows)
    state = lax.while_loop(more, body, state)
    carries = dict(zip(heads_rows, state[2:]))

    @pl.when(state[1] < CARRY_STOP)
    def _():
        meta_mask = lax.broadcasted_iota(jnp.int32, (sub, LANES), 1) < n_meta
        tri_m = tri_ref[0:LANES, 0:LANES]
        run([(h, a, km_ref[...], vm_ref[...], meta_mask, tri_m) for (h, a) in heads_rows], carries)

    o_ref[0] = jnp.where(lane < HEAD_DIM, acc_ref[0], acc_ref[1]).astype(o_ref.dtype)


def _sb_attn(qkv, qkv_meta, n_meta, tq, sub):
    b, s, w3 = qkv.shape
    width = w3 // 3
    pairs = width // LANES
    assert s % tq == 0 and tq % sub == 0 and sub % LANES == 0
    tri = (jnp.arange(sub)[:, None] >= jnp.arange(sub)[None, :]).astype(BF16)
    return pl.pallas_call(
        functools.partial(_sb_kernel, tq=tq, sub=sub, n_meta=n_meta),
        out_shape=jax.ShapeDtypeStruct((b, s, width), BF16),
        grid=(b, pairs, s // tq),
        in_specs=[
            pl.BlockSpec((1, tq, LANES), lambda bi, p, qi: (bi, qi, p)),
            pl.BlockSpec((1, s, LANES), lambda bi, p, qi: (bi, 0, pairs + p)),
            pl.BlockSpec((1, s, LANES), lambda bi, p, qi: (bi, 0, 2 * pairs + p)),
            pl.BlockSpec((LANES, LANES), lambda bi, p, qi: (0, pairs + p)),
            pl.BlockSpec((LANES, LANES), lambda bi, p, qi: (0, 2 * pairs + p)),
            pl.BlockSpec((sub, sub), lambda bi, p, qi: (0, 0)),
        ],
        out_specs=pl.BlockSpec((1, tq, LANES), lambda bi, p, qi: (bi, qi, p)),
        scratch_shapes=[pltpu.VMEM((2, tq, LANES), F32)],
        compiler_params=pltpu.CompilerParams(
            dimension_semantics=("parallel", "parallel", "arbitrary"),
            vmem_limit_bytes=VMEM_LIMIT),
        name="sb_attn",
    )(qkv, qkv, qkv, qkv_meta, qkv_meta, tri)


P_LORA = 1
P_EXACT = 3
P_MEAN = 2
P_SUM = 1


def _rwkv_kernel(ux_ref, um_ref, mu_ref, wlo_ref, dbase_ref, abase_ref, gup_ref, kk_ref, ka_ref,
                 rk_ref, lng_ref, lnb_ref, bd_ref, tril_ref, o_ref, st_ref, carry_ref, y_ref,
                 *, width):
    c = pl.program_id(1)
    R, C = RW_ROWS, CHUNK
    n_chunks = R // C
    pairs = width // LANES

    @pl.when(c == 0)
    def _():
        st_ref[...] = jnp.zeros_like(st_ref)
        carry_ref[...] = jnp.zeros_like(carry_ref)

    u = jnp.where(c == 0, um_ref[...], ux_ref[0])
    rowi = lax.broadcasted_iota(jnp.int32, u.shape, 0)
    prev = jnp.where(rowi == 0, carry_ref[...], pltpu.roll(u, 1, axis=0))
    carry_ref[...] = u[R - 1:R, :]
    us = u + mu_ref[...] * (prev - u)

    rr = us[:, 0:width]
    rk = us[:, width:2 * width]
    rv = us[:, 2 * width:3 * width]
    xwa = us[:, 3 * width:3 * width + LANES]
    xg = us[:, 3 * width + LANES:3 * width + 2 * LANES]

    lane_wa = lax.broadcasted_iota(jnp.int32, xwa.shape, 1)
    twa = jnp.where(lane_wa < HEAD_DIM, jnp.tanh(xwa), xwa)
    lo = _mm(twa, wlo_ref[...], pa=P_LORA, pb=P_LORA)
    dec_pre = dbase_ref[...] + lo[:, 0:width]
    ld = -jnp.exp(-_softplus(-dec_pre) - 0.5)
    a = jax.nn.sigmoid(abase_ref[...] + lo[:, width:2 * width])
    g = _mm(jax.nn.sigmoid(xg), gup_ref[...], pa=P_LORA, pb=P_LORA)

    bd = bd_ref[...]
    kkr = rk * kk_ref[...]
    ss = _mm(kkr * kkr, bd, pa=P_SUM)
    kk = kkr / jnp.maximum(jnp.sqrt(ss), KK_EPS)
    k = rk * (1.0 + (a - 1.0) * ka_ref[...])
    beta = kk * a

    cs = _mm(tril_ref[...], ld, pb=P_EXACT)
    cs_tot = jnp.concatenate(
        [jnp.broadcast_to(cs[(j + 1) * C - 1:(j + 1) * C, :], (C, width)) for j in range(n_chunks)], axis=0)
    d_last = [jnp.exp(cs[(j + 1) * C - 1:(j + 1) * C, :]) for j in range(n_chunks)]
    e_neg = jnp.exp(-cs)
    e_last = jnp.exp(cs_tot - cs)
    a_t = (-kk * jnp.exp(cs - ld)).astype(BF16)
    r_t = rr * jnp.exp(cs)
    b_t = (beta * e_neg).astype(BF16)
    k_t = (k * e_neg).astype(BF16)
    b_h = (beta * e_last).astype(BF16)
    k_h = (k * e_last).astype(BF16)
    v_b = rv.astype(BF16)

    ri = lax.broadcasted_iota(jnp.int32, (R, R), 0)
    ci = lax.broadcasted_iota(jnp.int32, (R, R), 1)
    same = (ri // C) == (ci // C)
    strict = same & (ci < ri)
    incl = same & (ci <= ri)
    eye = ci == ri
    head0 = (lax.broadcasted_iota(jnp.int32, (R, LANES), 1) < HEAD_DIM)
    head0_2 = jnp.concatenate([head0, head0], axis=1)
    si = lax.broadcasted_iota(jnp.int32, (LANES, LANES), 0)
    sj = lax.broadcasted_iota(jnp.int32, (LANES, LANES), 1)
    st_same = (si // HEAD_DIM) == (sj // HEAD_DIM)
    st_eye = si == sj
    zero_b = jnp.zeros((R, LANES), BF16)

    heads = [(p, h) for p in range(pairs) for h in range(2)]
    sls = [slice(p * LANES, (p + 1) * LANES) for p in range(pairs)]
    hmask = (head0, ~head0)
    r_b = r_t.astype(BF16)

    def packed(x_bd):
        out = x_bd[0:C]
        for j in range(1, n_chunks):
            out = out + x_bd[j * C:(j + 1) * C]
        return out

    def block_diag(x_p):
        return jnp.where(same, jnp.concatenate([x_p.astype(BF16)] * n_chunks, axis=0), jnp.zeros((R, R), BF16))

    l_ab, l_ak, p_rb, p_rk = {}, {}, {}, {}
    for (p, h) in heads:
        Ah = jnp.where(hmask[h], a_t[:, sls[p]], zero_b)
        Rh = jnp.where(hmask[h], r_b[:, sls[p]], zero_b)
        l_ab[p, h] = packed(jnp.where(strict, _bdot(Ah, b_t[:, sls[p]], NT), 0.0))
        l_ak[p, h] = jnp.where(strict, _bdot(Ah, k_t[:, sls[p]], NT), 0.0).astype(BF16)
        p_rb[p, h] = jnp.where(incl, _bdot(Rh, b_t[:, sls[p]], NT), 0.0).astype(BF16)
        p_rk[p, h] = jnp.where(incl, _bdot(Rh, k_t[:, sls[p]], NT), 0.0).astype(BF16)

    eye_p = packed(jnp.where(eye, 1.0, 0.0))
    ti = {ph: eye_p + l_ab[ph] for ph in heads}
    pw = dict(l_ab)
    pw_bd = {ph: block_diag(pw[ph]) for ph in heads}
    for _ in range(5):
        for ph in heads:
            pw[ph] = _bdot(pw[ph], pw_bd[ph])
            pw_bd[ph] = block_diag(pw[ph])
            ti[ph] = ti[ph] + _bdot(ti[ph], pw_bd[ph])
    t_inv = {ph: block_diag(ti[ph]) for ph in heads}

    lv = {ph: _bdot(l_ak[ph], v_b[:, sls[ph[0]]]) for ph in heads}
    wu, ya, yb = {}, {}, {}
    for p in range(pairs):
        lv2 = jnp.where(head0, lv[p, 0], lv[p, 1]).astype(BF16)
        rhs = jnp.concatenate([a_t[:, sls[p]], lv2], axis=1)
        wu[p] = jnp.where(head0_2, _bdot(t_inv[p, 0], rhs), _bdot(t_inv[p, 1], rhs)).astype(BF16)
    for p in range(pairs):
        pwu = jnp.where(head0_2, _bdot(p_rb[p, 0], wu[p]), _bdot(p_rb[p, 1], wu[p]))
        pv = jnp.where(head0, _bdot(p_rk[p, 0], v_b[:, sls[p]]), _bdot(p_rk[p, 1], v_b[:, sls[p]]))
        ya[p] = (r_t[:, sls[p]] + pwu[:, :LANES]).astype(BF16)
        yb[p] = pwu[:, LANES:] + pv

    gm, hm_ = {}, {}
    for j in range(n_chunks):
        rs = slice(j * C, (j + 1) * C)
        for p in range(pairs):
            gh = _bdot(b_h[rs, sls[p]], wu[p][rs], TN)
            kv = _bdot(k_h[rs, sls[p]], v_b[rs, sls[p]], TN)
            dl = jnp.broadcast_to(d_last[j][:, sls[p]], (LANES, LANES))
            gm[p, j] = (jnp.where(st_same, gh[:, :LANES], 0.0) + jnp.where(st_eye, dl, 0.0)).astype(BF16)
            hm_[p, j] = jnp.where(st_same, gh[:, LANES:] + kv, 0.0)

    m = {p: st_ref[p] for p in range(pairs)}
    for j in range(n_chunks):
        rs = slice(j * C, (j + 1) * C)
        for p in range(pairs):
            mb = m[p].astype(BF16)
            y_ref[rs, sls[p]] = _bdot(ya[p][rs], mb) + yb[p][rs]
            m[p] = _bdot(gm[p, j], mb) + hm_[p, j]
    for p in range(pairs):
        st_ref[p] = m[p]

    y = y_ref[...]
    inv_n = 1.0 / HEAD_DIM
    mean = _mm(y, bd, pa=P_MEAN) * inv_n
    yc = y - mean
    var = _mm(yc * yc, bd, pa=P_SUM) * inv_n
    yn = yc * lax.rsqrt(var + GN_EPS) * lng_ref[...] + lnb_ref[...]
    bonus = _mm(rr * k * rk_ref[...], bd, pa=1) * rv
    o_ref[0] = ((yn + bonus) * g).astype(o_ref.dtype)


def _rwkv(u_x, u_meta, mu, w_lora, dbase, abase, gate_up, k_k, k_a, r_k, ln_g, ln_b):
    b, s, uw = u_x.shape
    width = dbase.shape[1]
    assert s % RW_ROWS == 0 and uw == 3 * width + 2 * LANES and width % LANES == 0
    nc = s // RW_ROWS + 1
    hid = jnp.arange(width) // HEAD_DIM
    bd = (hid[:, None] == hid[None, :]).astype(BF16)
    t = jnp.arange(RW_ROWS)
    tril = ((t[:, None] // CHUNK == t[None, :] // CHUNK) & (t[:, None] >= t[None, :])).astype(BF16)
    const = lambda shape: pl.BlockSpec(shape, lambda bi, c: tuple(0 for _ in shape))
    return pl.pallas_call(
        functools.partial(_rwkv_kernel, width=width),
        out_shape=jax.ShapeDtypeStruct((b, s, width), BF16),
        grid=(b, nc),
        in_specs=[
            pl.BlockSpec((1, RW_ROWS, uw), lambda bi, c: (bi, jnp.maximum(c - 1, 0), 0)),
            const((RW_ROWS, uw)), const((1, uw)), const((LANES, 2 * width)), const((1, width)),
            const((1, width)), const((LANES, width)), const((1, width)), const((1, width)),
            const((1, width)), const((1, width)), const((1, width)), const((width, width)),
            const((RW_ROWS, RW_ROWS)),
        ],
        out_specs=pl.BlockSpec((1, RW_ROWS, width), lambda bi, c: (bi, jnp.maximum(c - 1, 0), 0)),
        scratch_shapes=[
            pltpu.VMEM((width // LANES, LANES, LANES), F32),
            pltpu.VMEM((1, uw), F32),
            pltpu.VMEM((RW_ROWS, width), F32),
        ],
        compiler_params=pltpu.CompilerParams(
            dimension_semantics=("parallel", "arbitrary"), vmem_limit_bytes=VMEM_LIMIT),
        name="rwkv",
    )(u_x, u_meta, mu, w_lora, dbase, abase, gate_up, k_k, k_a, r_k, ln_g, ln_b, bd, tril)


def _merge_ffn_kernel(x_ref, osb_ref, orw_ref, gsb_ref, grw_ref, wsb_ref, wrw_ref, wout_ref, gmix_ref,
                      gpre_ref, wg_ref, wu_ref, wd_ref, gpost_ref, o_ref, *, ff_chunks):
    a = jnp.dot(osb_ref[...], wsb_ref[...], preferred_element_type=F32)
    b = jnp.dot(orw_ref[...], wrw_ref[...], preferred_element_type=F32)
    merged = gsb_ref[...].astype(F32) * a + grw_ref[...].astype(F32) * b
    y = jnp.dot(merged.astype(BF16), wout_ref[...], preferred_element_type=F32)
    h1 = x_ref[...] + _rms(y, gmix_ref[...])
    xn = _rms(h1, gpre_ref[...]).astype(BF16)
    f = None
    for c0, c1 in ff_chunks:
        gate = jnp.dot(xn, wg_ref[:, c0:c1], preferred_element_type=F32)
        up = jnp.dot(xn, wu_ref[:, c0:c1], preferred_element_type=F32)
        act = (gate * jax.nn.sigmoid(gate) * up).astype(BF16)
        t = jnp.dot(act, wd_ref[c0:c1, :], preferred_element_type=F32)
        f = t if f is None else f + t
    o_ref[...] = h1 + _rms(f, gpost_ref[...])


def _merge_ffn(x2d, o_sb, o_rw, gates, w_sb, w_rw, w_out, g_mix, g_pre, w_gate, w_up, w_down, g_post, tm):
    m, d = x2d.shape
    width = o_sb.shape[1]
    ff = w_gate.shape[1]
    assert m % tm == 0 and ff % LANES == 0
    edges = list(range(0, ff, FF_CHUNK)) + [ff]
    ff_chunks = tuple(zip(edges[:-1], edges[1:]))
    row = lambda cols, jb=0: pl.BlockSpec((tm, cols), lambda i: (i, jb))
    return pl.pallas_call(
        functools.partial(_merge_ffn_kernel, ff_chunks=ff_chunks),
        out_shape=jax.ShapeDtypeStruct((m, d), F32),
        grid=(m // tm,),
        in_specs=[row(d), row(width), row(width), row(d, 0), row(d, 1),
                  _resident((width, d)), _resident((width, d)), _resident((d, d)), _resident((1, d)),
                  _resident((1, d)), _resident((d, ff)), _resident((d, ff)), _resident((ff, d)),
                  _resident((1, d))],
        out_specs=row(d),
        compiler_params=pltpu.CompilerParams(
            dimension_semantics=("parallel",), vmem_limit_bytes=VMEM_LIMIT),
        name="merge_ffn",
    )(x2d, o_sb, o_rw, gates, gates, w_sb, w_rw, w_out, g_mix, g_pre, w_gate, w_up, w_down, g_post)


def _pick(n, prefs):
    for t in prefs:
        if n % t == 0:
            return t
    return n


def kernel(x, meta_tokens, norm_mix_pre, norm_mix_post, w_in, rw_shift_mu, rw_decay_up, rw_decay_base,
           rw_aaa_up, rw_aaa_base, rw_gate_up, rw_k_k, rw_k_a, rw_r_k, rw_ln_gain, rw_ln_bias,
           w_branch_sb, w_branch_rw, w_out, norm_ffn_pre, norm_ffn_post, w_ffn_gate, w_ffn_up,
           w_ffn_down):
    b, s, d = x.shape
    n_meta = meta_tokens.shape[0]
    depth = w_in.shape[0]
    assert depth == 1, "meta rows are only carried through the mixer of a single layer"
    width = w_branch_sb.shape[1]
    dlora = rw_decay_up.shape[1]
    alora = rw_aaa_up.shape[1]
    glora = rw_gate_up.shape[1]
    assert dlora + alora == LANES and glora == LANES and n_meta <= RW_ROWS
    l = 0
    m = b * s
    x2d = x.reshape(m, d)

    c_sb, c_rw = 3 * width, 3 * width + 3 * width + 2 * LANES
    w_in_b = w_in[l].astype(BF16)
    g_pre = norm_mix_pre[l][None, :]

    qkv, u_rw, gates = _proj(x2d, g_pre, w_in_b, c_sb, c_rw, _pick(m, (1024, 512, 256, 128)), "proj")
    meta_pad = jnp.zeros((LANES, d), F32).at[:n_meta].set(meta_tokens.astype(F32))
    qkv_meta, rw_meta, _ = _proj(meta_pad, g_pre, w_in_b, c_sb, c_rw, LANES, "proj_meta")
    row_ok = (jnp.arange(LANES) < n_meta)[:, None]
    qkv_meta = jnp.where(row_ok, qkv_meta, jnp.zeros_like(qkv_meta))
    u_meta = jnp.zeros((RW_ROWS, c_rw - c_sb), F32).at[RW_ROWS - n_meta:].set(rw_meta[:n_meta])

    tq = _pick(s, (512, 256, 128))
    o_sb = _sb_attn(qkv.reshape(b, s, c_sb), qkv_meta, n_meta, tq, min(tq, 256))

    w_lora = jnp.zeros((LANES, 2 * width), F32)
    w_lora = w_lora.at[:dlora, :width].set(rw_decay_up[l]).at[dlora:, width:].set(rw_aaa_up[l])
    vec = lambda p: p[l].reshape(1, -1).astype(F32)
    o_rw = _rwkv(u_rw.reshape(b, s, c_rw - c_sb), u_meta, vec(rw_shift_mu), w_lora,
                 vec(rw_decay_base), vec(rw_aaa_base), rw_gate_up[l].astype(F32), vec(rw_k_k),
                 vec(rw_k_a), vec(rw_r_k), vec(rw_ln_gain), vec(rw_ln_bias))

    out = _merge_ffn(x2d, o_sb.reshape(m, width), o_rw.reshape(m, width), gates,
                     w_branch_sb[l].astype(BF16), w_branch_rw[l].astype(BF16), w_out[l].astype(BF16),
                     norm_mix_post[l][None, :], norm_ffn_pre[l][None, :], w_ffn_gate[l].astype(BF16),
                     w_ffn_up[l].astype(BF16), w_ffn_down[l].astype(BF16), norm_ffn_post[l][None, :],
                     _pick(m, (512, 256, 128)))
    return out.reshape(b, s, d)
```

```python
import functools

import jax
import jax.numpy as jnp
from jax import lax
from jax.experimental import pallas as pl
from jax.experimental.pallas import tpu as pltpu

HEAD_DIM = 64
RMS_EPS = 1e-6
GN_EPS = 64e-5
KK_EPS = 1e-12
LANES = 128
CHUNK = 64
RW_ROWS = 256
FF_CHUNK = 1024
VMEM_LIMIT = 56 * 1024 * 1024
LOG2E = 1.4426950408889634
MASKED = 1e30
CARRY_STOP = 64.0
TILES_AHEAD = 8

F32 = jnp.float32
BF16 = jnp.bfloat16

NN = (((1,), (0,)), ((), ()))
NT = (((1,), (1,)), ((), ()))
TN = (((0,), (0,)), ((), ()))


def _split(x, n):
    if x.dtype == BF16:
        return [x]
    parts = []
    r = x
    for i in range(n):
        p = r.astype(BF16)
        parts.append(p)
        if i + 1 < n:
            r = r - p.astype(F32)
    return parts


def _mm(a, b, dims=NN, pa=1, pb=1):
    a_parts = _split(a, pa)
    b_parts = _split(b, pb)
    order = max(len(a_parts), len(b_parts))
    acc = None
    for i, ai in enumerate(a_parts):
        for j, bj in enumerate(b_parts):
            if i + j >= order:
                continue
            t = lax.dot_general(ai, bj, dims, preferred_element_type=F32)
            acc = t if acc is None else acc + t
    return acc


def _bdot(a, b, dims=NN):
    return lax.dot_general(a.astype(BF16), b.astype(BF16), dims, preferred_element_type=F32)


def _softplus(z):
    return jnp.maximum(z, 0.0) + jnp.log1p(jnp.exp(-jnp.abs(z)))


def _rms(x, g):
    ms = jnp.mean(x * x, axis=-1, keepdims=True)
    return x * lax.rsqrt(ms + RMS_EPS) * g


def _resident(shape):
    return pl.BlockSpec(shape, lambda i: (0,) * len(shape), pipeline_mode=pl.Buffered(1))


def _proj_kernel(x_ref, g_ref, w_ref, qkv_ref, rw_ref, gate_ref, *, c_sb, c_rw):
    xn = _rms(x_ref[...], g_ref[...]).astype(BF16)
    qkv_ref[...] = jnp.dot(xn, w_ref[:, 0:c_sb], preferred_element_type=F32).astype(qkv_ref.dtype)
    rw_ref[...] = jnp.dot(xn, w_ref[:, c_sb:c_rw], preferred_element_type=F32)
    gate_ref[...] = jax.nn.sigmoid(
        jnp.dot(xn, w_ref[:, c_rw:], preferred_element_type=F32)).astype(gate_ref.dtype)


def _proj(x2d, gain, w, c_sb, c_rw, tm, name):
    m, d = x2d.shape
    n = w.shape[1]
    assert m % tm == 0
    row = lambda cols: pl.BlockSpec((tm, cols), lambda i: (i, 0))
    return pl.pallas_call(
        functools.partial(_proj_kernel, c_sb=c_sb, c_rw=c_rw),
        out_shape=(jax.ShapeDtypeStruct((m, c_sb), BF16),
                   jax.ShapeDtypeStruct((m, c_rw - c_sb), F32),
                   jax.ShapeDtypeStruct((m, n - c_rw), BF16)),
        grid=(m // tm,),
        in_specs=[row(d), _resident((1, d)), _resident((d, n))],
        out_specs=(row(c_sb), row(c_rw - c_sb), row(n - c_rw)),
        compiler_params=pltpu.CompilerParams(
            dimension_semantics=("parallel",), vmem_limit_bytes=VMEM_LIMIT),
        name=name,
    )(x2d, gain, w)


def _sb_kernel(q_ref, k_ref, v_ref, km_ref, vm_ref, tri_ref, o_ref, acc_ref, *, tq, sub, n_meta):
    qi = pl.program_id(2)
    lane = lax.broadcasted_iota(jnp.int32, (tq, LANES), 1)
    q2 = (q_ref[0].astype(F32) * (HEAD_DIM ** -0.5 * LOG2E)).astype(BF16)
    zero = jnp.zeros_like(q2)
    qh = (jnp.where(lane < HEAD_DIM, q2, zero), jnp.where(lane >= HEAD_DIM, q2, zero))
    tri = tri_ref[...]

    n_row = tq // sub
    assert n_row in (1, 2), "the block schedule below covers one or two row groups"
    rows = lambda a: slice(a * sub, (a + 1) * sub)

    def start(h, a, kblk, mask, tri_b):
        z = lax.dot_general(qh[h][rows(a)], kblk, NT, preferred_element_type=F32)
        if mask is not None:
            z = jnp.where(mask, z, -MASKED)
        sp = jnp.maximum(z, 0.0) + jnp.log2(1.0 + jnp.exp2(-jnp.abs(z)))
        return z, jnp.dot(sp.astype(BF16), tri_b, preferred_element_type=F32)

    def finish(h, a, zc, carry, vblk):
        z, cs = zc
        c = cs + carry
        acc_ref[h, rows(a), :] += jnp.dot(jnp.exp2(z - c).astype(BF16), vblk, preferred_element_type=F32)
        return c[:, 0:1]

    def run(tiles, carries):
        carries = dict(carries)
        pending = []
        for (h, a, kblk, vblk, mask, tri_b) in tiles:
            pending.append((h, a, start(h, a, kblk, mask, tri_b), vblk))
            if len(pending) > TILES_AHEAD:
                ph, pa, pzc, pv = pending.pop(0)
                carries[ph, pa] = finish(ph, pa, pzc, carries[ph, pa], pv)
        for ph, pa, pzc, pv in pending:
            carries[ph, pa] = finish(ph, pa, pzc, carries[ph, pa], pv)
        return carries

    def min_carry(carries):
        m = None
        for c in carries.values():
            m = c if m is None else jnp.minimum(m, c)
        return jnp.min(m)

    acc_ref[...] = jnp.zeros_like(acc_ref)
    keys_at = lambda s0: k_ref[0, pl.ds(pl.multiple_of(s0, sub), sub), :]
    vals_at = lambda s0: v_ref[0, pl.ds(pl.multiple_of(s0, sub), sub), :]
    ri = lax.broadcasted_iota(jnp.int32, (sub, sub), 0)
    ci = lax.broadcasted_iota(jnp.int32, (sub, sub), 1)
    causal = ci < ri
    d0 = qi * tq
    heads_rows = [(h, a) for a in range(n_row) for h in range(2)]

    tiles = [(h, a, keys_at(d0 + a * sub), vals_at(d0 + a * sub), causal, tri) for (h, a) in heads_rows]
    for (h, a) in heads_rows:
        if a == 0:
            s0 = jnp.maximum(d0 - sub, 0)
            tiles.append((h, a, keys_at(s0), vals_at(s0), jnp.broadcast_to(qi > 0, (sub, sub)), tri))
        else:
            s0 = d0 + (a - 1) * sub
            tiles.append((h, a, keys_at(s0), vals_at(s0), None, tri))
    carries = run(tiles, {ha: jnp.zeros((sub, 1), F32) for ha in heads_rows})

    n_left = qi * n_row

    def more(state):
        return jnp.logical_and(state[0] < n_left, state[1] < CARRY_STOP)

    def body(state):
        i = state[0]
        carries = dict(zip(heads_rows, state[2:]))
        s0 = (n_left - 1 - i) * sub
        kblk, vblk = keys_at(s0), vals_at(s0)
        tiles = []
        for (h, a) in heads_rows:
            fresh = jnp.broadcast_to(i >= 1, (sub, sub)) if a == 0 else None
            tiles.append((h, a, kblk, vblk, fresh, tri))
        carries = run(tiles, carries)
        return (i + 1, min_carry(carries)) + tuple(carries[ha] for ha in heads_rows)

    state = (jnp.int32(0), min_carry(carries)) + tuple(carries[ha] for ha in heads_rows)
    state = lax.while_loop(more, body, state)
    carries = dict(zip(heads_rows, state[2:]))

    @pl.when(state[1] < CARRY_STOP)
    def _():
        meta_mask = lax.broadcasted_iota(jnp.int32, (sub, LANES), 1) < n_meta
        tri_m = tri_ref[0:LANES, 0:LANES]
        run([(h, a, km_ref[...], vm_ref[...], meta_mask, tri_m) for (h, a) in heads_rows], carries)

    o_ref[0] = jnp.where(lane < HEAD_DIM, acc_ref[0], acc_ref[1]).astype(o_ref.dtype)


def _sb_attn(qkv, qkv_meta, n_meta, tq, sub):
    b, s, w3 = qkv.shape
    width = w3 // 3
    pairs = width // LANES
    assert s % tq == 0 and tq % sub == 0 and sub % LANES == 0
    tri = (jnp.arange(sub)[:, None] >= jnp.arange(sub)[None, :]).astype(BF16)
    return pl.pallas_call(
        functools.partial(_sb_kernel, tq=tq, sub=sub, n_meta=n_meta),
        out_shape=jax.ShapeDtypeStruct((b, s, width), BF16),
        grid=(b, pairs, s // tq),
        in_specs=[
            pl.BlockSpec((1, tq, LANES), lambda bi, p, qi: (bi, qi, p)),
            pl.BlockSpec((1, s, LANES), lambda bi, p, qi: (bi, 0, pairs + p)),
            pl.BlockSpec((1, s, LANES), lambda bi, p, qi: (bi, 0, 2 * pairs + p)),
            pl.BlockSpec((LANES, LANES), lambda bi, p, qi: (0, pairs + p)),
            pl.BlockSpec((LANES, LANES), lambda bi, p, qi: (0, 2 * pairs + p)),
            pl.BlockSpec((sub, sub), lambda bi, p, qi: (0, 0)),
        ],
        out_specs=pl.BlockSpec((1, tq, LANES), lambda bi, p, qi: (bi, qi, p)),
        scratch_shapes=[pltpu.VMEM((2, tq, LANES), F32)],
        compiler_params=pltpu.CompilerParams(
            dimension_semantics=("parallel", "parallel", "arbitrary"),
            vmem_limit_bytes=VMEM_LIMIT),
        name="sb_attn",
    )(qkv, qkv, qkv, qkv_meta, qkv_meta, tri)


P_LORA = 1
P_EXACT = 3
P_MEAN = 2
P_SUM = 1
GROUP_PAIRS = 4
STAGE_DELAY = 0


def _rwkv_kernel(ux_ref, um_ref, mu_ref, wlo_ref, dbase_ref, abase_ref, gup_ref, kk_ref, ka_ref,
                 rk_ref, lng_ref, lnb_ref, bd_ref, tril_ref, o_ref, st_ref, carry_ref, y_ref,
                 *, width):
    c = pl.program_id(1)
    R, C = RW_ROWS, CHUNK
    n_chunks = R // C
    gw = GROUP_PAIRS * LANES
    n_groups = width // gw

    @pl.when(c == 0)
    def _():
        st_ref[...] = jnp.zeros_like(st_ref)
        carry_ref[...] = jnp.zeros_like(carry_ref)

    u = jnp.where(c == 0, um_ref[...], ux_ref[0])
    rowi = lax.broadcasted_iota(jnp.int32, u.shape, 0)
    prev = jnp.where(rowi == 0, carry_ref[...], pltpu.roll(u, 1, axis=0))
    carry_ref[...] = u[R - 1:R, :]
    us = u + mu_ref[...] * (prev - u)

    xwa = us[:, 3 * width:3 * width + LANES]
    xg = us[:, 3 * width + LANES:3 * width + 2 * LANES]
    lane_wa = lax.broadcasted_iota(jnp.int32, xwa.shape, 1)
    twa = jnp.where(lane_wa < HEAD_DIM, jnp.tanh(xwa), xwa)
    sig_g = jax.nn.sigmoid(xg)
    bd = bd_ref[...]
    tril = tril_ref[...]

    ri = lax.broadcasted_iota(jnp.int32, (R, R), 0)
    ci = lax.broadcasted_iota(jnp.int32, (R, R), 1)
    same = (ri // C) == (ci // C)
    strict = same & (ci < ri)
    incl = same & (ci <= ri)
    eye = ci == ri
    head0 = (lax.broadcasted_iota(jnp.int32, (R, LANES), 1) < HEAD_DIM)
    head0_2 = jnp.concatenate([head0, head0], axis=1)
    hmask = (head0, ~head0)
    si = lax.broadcasted_iota(jnp.int32, (LANES, LANES), 0)
    sj = lax.broadcasted_iota(jnp.int32, (LANES, LANES), 1)
    st_same = (si // HEAD_DIM) == (sj // HEAD_DIM)
    st_eye = si == sj
    zero_b = jnp.zeros((R, LANES), BF16)

    def packed(x_bd):
        out = x_bd[0:C]
        for j in range(1, n_chunks):
            out = out + x_bd[j * C:(j + 1) * C]
        return out

    def block_diag(x_p):
        return jnp.where(same, jnp.concatenate([x_p.astype(BF16)] * n_chunks, axis=0), jnp.zeros((R, R), BF16))

    eye_p = packed(jnp.where(eye, 1.0, 0.0))

    def stages(g):
        base = g * gw
        cols = slice(base, base + gw)
        at = lambda off: slice(off + base, off + base + gw)
        heads = [(q, h) for q in range(GROUP_PAIRS) for h in range(2)]
        sls = [slice(q * LANES, (q + 1) * LANES) for q in range(GROUP_PAIRS)]

        rr, rk, rv = us[:, at(0)], us[:, at(width)], us[:, at(2 * width)]
        dec_pre = dbase_ref[:, cols] + _mm(twa, wlo_ref[:, at(0)], pa=P_LORA, pb=P_LORA)
        ld = -jnp.exp(-_softplus(-dec_pre) - 0.5)
        a = jax.nn.sigmoid(abase_ref[:, cols] + _mm(twa, wlo_ref[:, at(width)], pa=P_LORA, pb=P_LORA))
        gate = _mm(sig_g, gup_ref[:, cols], pa=P_LORA, pb=P_LORA)
        kkr = rk * kk_ref[:, cols]
        ss = _mm(kkr * kkr, bd, pa=P_SUM)
        kk = kkr / jnp.maximum(jnp.sqrt(ss), KK_EPS)
        k = rk * (1.0 + (a - 1.0) * ka_ref[:, cols])
        beta = kk * a
        yield

        cs = _mm(tril, ld, pb=P_EXACT)
        cs_tot = jnp.concatenate(
            [jnp.broadcast_to(cs[(j + 1) * C - 1:(j + 1) * C, :], (C, gw)) for j in range(n_chunks)], axis=0)
        d_last = [jnp.exp(cs[(j + 1) * C - 1:(j + 1) * C, :]) for j in range(n_chunks)]
        e_neg = jnp.exp(-cs)
        e_last = jnp.exp(cs_tot - cs)
        a_t = (-kk * jnp.exp(cs - ld)).astype(BF16)
        r_t = rr * jnp.exp(cs)
        r_b = r_t.astype(BF16)
        b_t = (beta * e_neg).astype(BF16)
        k_t = (k * e_neg).astype(BF16)
        b_h = (beta * e_last).astype(BF16)
        k_h = (k * e_last).astype(BF16)
        v_b = rv.astype(BF16)
        yield

        l_ab, l_ak, p_rb, p_rk = {}, {}, {}, {}
        for (q, h) in heads:
            Ah = jnp.where(hmask[h], a_t[:, sls[q]], zero_b)
            Rh = jnp.where(hmask[h], r_b[:, sls[q]], zero_b)
            l_ab[q, h] = jnp.where(strict, _bdot(Ah, b_t[:, sls[q]], NT), 0.0)
            l_ak[q, h] = jnp.where(strict, _bdot(Ah, k_t[:, sls[q]], NT), 0.0).astype(BF16)
            p_rb[q, h] = jnp.where(incl, _bdot(Rh, b_t[:, sls[q]], NT), 0.0).astype(BF16)
            p_rk[q, h] = jnp.where(incl, _bdot(Rh, k_t[:, sls[q]], NT), 0.0).astype(BF16)
        yield

        pw_bd = {qh: l_ab[qh].astype(BF16) for qh in heads}
        pw = {qh: packed(l_ab[qh]) for qh in heads}
        ti = {qh: eye_p + pw[qh] for qh in heads}
        for qh in heads:
            pw[qh] = _bdot(pw[qh], pw_bd[qh])
        yield
        n_levels = 5
        for lvl in range(n_levels):
            last = lvl == n_levels - 1
            for qh in heads:
                pw_bd[qh] = block_diag(pw[qh])
                if last:
                    ti[qh] = ti[qh] + _bdot(ti[qh], pw_bd[qh])
                else:
                    both = _bdot(jnp.concatenate([ti[qh], pw[qh]], axis=0), pw_bd[qh])
                    ti[qh] = ti[qh] + both[0:C]
                    pw[qh] = both[C:2 * C]
            yield

        t_inv = {qh: block_diag(ti[qh]) for qh in heads}
        lv = {qh: _bdot(l_ak[qh], v_b[:, sls[qh[0]]]) for qh in heads}
        wu, ya, yb = {}, {}, {}
        for q in range(GROUP_PAIRS):
            lv2 = jnp.where(head0, lv[q, 0], lv[q, 1]).astype(BF16)
            rhs = jnp.concatenate([a_t[:, sls[q]], lv2], axis=1)
            wu[q] = jnp.where(head0_2, _bdot(t_inv[q, 0], rhs), _bdot(t_inv[q, 1], rhs)).astype(BF16)
        yield
        for q in range(GROUP_PAIRS):
            pwu = jnp.where(head0_2, _bdot(p_rb[q, 0], wu[q]), _bdot(p_rb[q, 1], wu[q]))
            pv = jnp.where(head0, _bdot(p_rk[q, 0], v_b[:, sls[q]]), _bdot(p_rk[q, 1], v_b[:, sls[q]]))
            ya[q] = (r_t[:, sls[q]] + pwu[:, :LANES]).astype(BF16)
            yb[q] = pwu[:, LANES:] + pv
        yield

        gm, hm_ = {}, {}
        for j in range(n_chunks):
            rs = slice(j * C, (j + 1) * C)
            for q in range(GROUP_PAIRS):
                gh = _bdot(b_h[rs, sls[q]], wu[q][rs], TN)
                kv = _bdot(k_h[rs, sls[q]], v_b[rs, sls[q]], TN)
                dl = jnp.broadcast_to(d_last[j][:, sls[q]], (LANES, LANES))
                gm[q, j] = (jnp.where(st_same, gh[:, :LANES], 0.0) + jnp.where(st_eye, dl, 0.0)).astype(BF16)
                hm_[q, j] = jnp.where(st_same, gh[:, LANES:] + kv, 0.0)
        yield

        m = {q: st_ref[g * GROUP_PAIRS + q] for q in range(GROUP_PAIRS)}
        for j in range(n_chunks):
            rs = slice(j * C, (j + 1) * C)
            for q in range(GROUP_PAIRS):
                mb = m[q].astype(BF16)
                y_ref[rs, base + q * LANES:base + (q + 1) * LANES] = _bdot(ya[q][rs], mb) + yb[q][rs]
                m[q] = _bdot(gm[q, j], mb) + hm_[q, j]
        for q in range(GROUP_PAIRS):
            st_ref[g * GROUP_PAIRS + q] = m[q]
        yield

        y = y_ref[:, cols]
        inv_n = 1.0 / HEAD_DIM
        mean = _mm(y, bd, pa=P_MEAN) * inv_n
        yc = y - mean
        var = _mm(yc * yc, bd, pa=P_SUM) * inv_n
        yn = yc * lax.rsqrt(var + GN_EPS) * lng_ref[:, cols] + lnb_ref[:, cols]
        bonus = _mm(rr * k * rk_ref[:, cols], bd, pa=1) * rv
        o_ref[0, :, cols] = ((yn + bonus) * gate).astype(o_ref.dtype)

    gens = [stages(g) for g in range(n_groups)]
    live = [True] * n_groups
    tick = 0
    while any(live):
        for g in range(n_groups):
            if live[g] and tick >= g * STAGE_DELAY:
                live[g] = next(gens[g], "done") != "done"
        tick += 1


def _rwkv(u_x, u_meta, mu, w_lora, dbase, abase, gate_up, k_k, k_a, r_k, ln_g, ln_b):
    b, s, uw = u_x.shape
    width = dbase.shape[1]
    assert s % RW_ROWS == 0 and uw == 3 * width + 2 * LANES and width % LANES == 0
    nc = s // RW_ROWS + 1
    gw = GROUP_PAIRS * LANES
    assert width % gw == 0
    hid = jnp.arange(gw) // HEAD_DIM
    bd = (hid[:, None] == hid[None, :]).astype(BF16)
    t = jnp.arange(RW_ROWS)
    tril = ((t[:, None] // CHUNK == t[None, :] // CHUNK) & (t[:, None] >= t[None, :])).astype(BF16)
    const = lambda shape: pl.BlockSpec(shape, lambda bi, c: tuple(0 for _ in shape))
    return pl.pallas_call(
        functools.partial(_rwkv_kernel, width=width),
        out_shape=jax.ShapeDtypeStruct((b, s, width), BF16),
        grid=(b, nc),
        in_specs=[
            pl.BlockSpec((1, RW_ROWS, uw), lambda bi, c: (bi, jnp.maximum(c - 1, 0), 0)),
            const((RW_ROWS, uw)), const((1, uw)), const((LANES, 2 * width)), const((1, width)),
            const((1, width)), const((LANES, width)), const((1, width)), const((1, width)),
            const((1, width)), const((1, width)), const((1, width)), const((gw, gw)),
            const((RW_ROWS, RW_ROWS)),
        ],
        out_specs=pl.BlockSpec((1, RW_ROWS, width), lambda bi, c: (bi, jnp.maximum(c - 1, 0), 0)),
        scratch_shapes=[
            pltpu.VMEM((width // LANES, LANES, LANES), F32),
            pltpu.VMEM((1, uw), F32),
            pltpu.VMEM((RW_ROWS, width), F32),
        ],
        compiler_params=pltpu.CompilerParams(
            dimension_semantics=("parallel", "arbitrary"), vmem_limit_bytes=VMEM_LIMIT),
        name="rwkv",
    )(u_x, u_meta, mu, w_lora, dbase, abase, gate_up, k_k, k_a, r_k, ln_g, ln_b, bd, tril)


def _merge_ffn_kernel(x_ref, osb_ref, orw_ref, gsb_ref, grw_ref, wsb_ref, wrw_ref, wout_ref, gmix_ref,
                      gpre_ref, wg_ref, wu_ref, wd_ref, gpost_ref, o_ref, *, ff_chunks):
    a = jnp.dot(osb_ref[...], wsb_ref[...], preferred_element_type=F32)
    b = jnp.dot(orw_ref[...], wrw_ref[...], preferred_element_type=F32)
    merged = gsb_ref[...].astype(F32) * a + grw_ref[...].astype(F32) * b
    y = jnp.dot(merged.astype(BF16), wout_ref[...], preferred_element_type=F32)
    h1 = x_ref[...] + _rms(y, gmix_ref[...])
    xn = _rms(h1, gpre_ref[...]).astype(BF16)
    f = None
    for c0, c1 in ff_chunks:
        gate = jnp.dot(xn, wg_ref[:, c0:c1], preferred_element_type=F32)
        up = jnp.dot(xn, wu_ref[:, c0:c1], preferred_element_type=F32)
        act = (gate * jax.nn.sigmoid(gate) * up).astype(BF16)
        t = jnp.dot(act, wd_ref[c0:c1, :], preferred_element_type=F32)
        f = t if f is None else f + t
    o_ref[...] = h1 + _rms(f, gpost_ref[...])


def _merge_ffn(x2d, o_sb, o_rw, gates, w_sb, w_rw, w_out, g_mix, g_pre, w_gate, w_up, w_down, g_post, tm):
    m, d = x2d.shape
    width = o_sb.shape[1]
    ff = w_gate.shape[1]
    assert m % tm == 0 and ff % LANES == 0
    edges = list(range(0, ff, FF_CHUNK)) + [ff]
    ff_chunks = tuple(zip(edges[:-1], edges[1:]))
    row = lambda cols, jb=0: pl.BlockSpec((tm, cols), lambda i: (i, jb))
    return pl.pallas_call(
        functools.partial(_merge_ffn_kernel, ff_chunks=ff_chunks),
        out_shape=jax.ShapeDtypeStruct((m, d), F32),
        grid=(m // tm,),
        in_specs=[row(d), row(width), row(width), row(d, 0), row(d, 1),
                  _resident((width, d)), _resident((width, d)), _resident((d, d)), _resident((1, d)),
                  _resident((1, d)), _resident((d, ff)), _resident((d, ff)), _resident((ff, d)),
                  _resident((1, d))],
        out_specs=row(d),
        compiler_params=pltpu.CompilerParams(
            dimension_semantics=("parallel",), vmem_limit_bytes=VMEM_LIMIT),
        name="merge_ffn",
    )(x2d, o_sb, o_rw, gates, gates, w_sb, w_rw, w_out, g_mix, g_pre, w_gate, w_up, w_down, g_post)


def _pick(n, prefs):
    for t in prefs:
        if n % t == 0:
            return t
    return n


def kernel(x, meta_tokens, norm_mix_pre, norm_mix_post, w_in, rw_shift_mu, rw_decay_up, rw_decay_base,
           rw_aaa_up, rw_aaa_base, rw_gate_up, rw_k_k, rw_k_a, rw_r_k, rw_ln_gain, rw_ln_bias,
           w_branch_sb, w_branch_rw, w_out, norm_ffn_pre, norm_ffn_post, w_ffn_gate, w_ffn_up,
           w_ffn_down):
    b, s, d = x.shape
    n_meta = meta_tokens.shape[0]
    depth = w_in.shape[0]
    assert depth == 1, "meta rows are only carried through the mixer of a single layer"
    width = w_branch_sb.shape[1]
    dlora = rw_decay_up.shape[1]
    alora = rw_aaa_up.shape[1]
    glora = rw_gate_up.shape[1]
    assert dlora + alora == LANES and glora == LANES and n_meta <= RW_ROWS
    l = 0
    m = b * s
    x2d = x.reshape(m, d)

    c_sb, c_rw = 3 * width, 3 * width + 3 * width + 2 * LANES
    w_in_b = w_in[l].astype(BF16)
    g_pre = norm_mix_pre[l][None, :]

    qkv, u_rw, gates = _proj(x2d, g_pre, w_in_b, c_sb, c_rw, _pick(m, (1024, 512, 256, 128)), "proj")
    meta_pad = jnp.zeros((LANES, d), F32).at[:n_meta].set(meta_tokens.astype(F32))
    qkv_meta, rw_meta, _ = _proj(meta_pad, g_pre, w_in_b, c_sb, c_rw, LANES, "proj_meta")
    row_ok = (jnp.arange(LANES) < n_meta)[:, None]
    qkv_meta = jnp.where(row_ok, qkv_meta, jnp.zeros_like(qkv_meta))
    u_meta = jnp.zeros((RW_ROWS, c_rw - c_sb), F32).at[RW_ROWS - n_meta:].set(rw_meta[:n_meta])

    tq = _pick(s, (512, 256, 128))
    o_sb = _sb_attn(qkv.reshape(b, s, c_sb), qkv_meta, n_meta, tq, min(tq, 256))

    w_lora = jnp.zeros((LANES, 2 * width), F32)
    w_lora = w_lora.at[:dlora, :width].set(rw_decay_up[l]).at[dlora:, width:].set(rw_aaa_up[l])
    vec = lambda p: p[l].reshape(1, -1).astype(F32)
    o_rw = _rwkv(u_rw.reshape(b, s, c_rw - c_sb), u_meta, vec(rw_shift_mu), w_lora,
                 vec(rw_decay_base), vec(rw_aaa_base), rw_gate_up[l].astype(F32), vec(rw_k_k),
                 vec(rw_k_a), vec(rw_r_k), vec(rw_ln_gain), vec(rw_ln_bias))

    out = _merge_ffn(x2d, o_sb.reshape(m, width), o_rw.reshape(m, width), gates,
                     w_branch_sb[l].astype(BF16), w_branch_rw[l].astype(BF16), w_out[l].astype(BF16),
                     norm_mix_post[l][None, :], norm_ffn_pre[l][None, :], w_ffn_gate[l].astype(BF16),
                     w_ffn_up[l].astype(BF16), w_ffn_down[l].astype(BF16), norm_ffn_post[l][None, :],
                     _pick(m, (512, 256, 128)))
    return out.reshape(b, s, d)
```

```python
import functools

import jax
import jax.numpy as jnp
from jax import lax
from jax.experimental import pallas as pl
from jax.experimental.pallas import tpu as pltpu

HEAD_DIM = 64
RMS_EPS = 1e-6
GN_EPS = 64e-5
KK_EPS = 1e-12
LANES = 128
SUBLANES = 8
DECAY_SCALE = 0.6065306597126334
CHUNK = 64
RW_ROWS = 256
FF_CHUNK = 1024
VMEM_LIMIT = 56 * 1024 * 1024
LOG2E = 1.4426950408889634
MASKED = 1e30
CARRY_STOP = 64.0
TILES_AHEAD = 16

F32 = jnp.float32
BF16 = jnp.bfloat16

NN = (((1,), (0,)), ((), ()))
NT = (((1,), (1,)), ((), ()))
TN = (((0,), (0,)), ((), ()))


def _split(x, n):
    if x.dtype == BF16:
        return [x]
    parts = []
    r = x
    for i in range(n):
        p = r.astype(BF16)
        parts.append(p)
        if i + 1 < n:
            r = r - p.astype(F32)
    return parts


def _mm(a, b, dims=NN, pa=1, pb=1):
    a_parts = _split(a, pa)
    b_parts = _split(b, pb)
    order = max(len(a_parts), len(b_parts))
    acc = None
    for i, ai in enumerate(a_parts):
        for j, bj in enumerate(b_parts):
            if i + j >= order:
                continue
            t = lax.dot_general(ai, bj, dims, preferred_element_type=F32)
            acc = t if acc is None else acc + t
    return acc


def _bdot(a, b, dims=NN):
    return lax.dot_general(a.astype(BF16), b.astype(BF16), dims, preferred_element_type=F32)


def _rms(x, g):
    ms = jnp.mean(x * x, axis=-1, keepdims=True)
    return x * lax.rsqrt(ms + RMS_EPS) * g


def _resident(shape):
    return pl.BlockSpec(shape, lambda i: (0,) * len(shape), pipeline_mode=pl.Buffered(1))


def _proj_kernel(x_ref, g_ref, w_ref, qkv_ref, rw_ref, gate_ref, *, c_sb, c_rw):
    xn = _rms(x_ref[...], g_ref[...]).astype(BF16)
    qkv_ref[...] = jnp.dot(xn, w_ref[:, 0:c_sb], preferred_element_type=F32).astype(qkv_ref.dtype)
    rw_ref[...] = jnp.dot(xn, w_ref[:, c_sb:c_rw], preferred_element_type=F32)
    gate_ref[...] = jax.nn.sigmoid(
        jnp.dot(xn, w_ref[:, c_rw:], preferred_element_type=F32)).astype(gate_ref.dtype)


def _proj(x2d, gain, w, c_sb, c_rw, tm, name):
    m, d = x2d.shape
    n = w.shape[1]
    assert m % tm == 0
    row = lambda cols: pl.BlockSpec((tm, cols), lambda i: (i, 0))
    return pl.pallas_call(
        functools.partial(_proj_kernel, c_sb=c_sb, c_rw=c_rw),
        out_shape=(jax.ShapeDtypeStruct((m, c_sb), BF16),
                   jax.ShapeDtypeStruct((m, c_rw - c_sb), F32),
                   jax.ShapeDtypeStruct((m, n - c_rw), BF16)),
        grid=(m // tm,),
        in_specs=[row(d), _resident((1, d)), _resident((d, n))],
        out_specs=(row(c_sb), row(c_rw - c_sb), row(n - c_rw)),
        compiler_params=pltpu.CompilerParams(
            dimension_semantics=("parallel",), vmem_limit_bytes=VMEM_LIMIT),
        name=name,
    )(x2d, gain, w)


def _sb_kernel(q_ref, k_ref, v_ref, km_ref, vm_ref, tri_ref, o_ref, acc_ref, *, tq, sub, n_meta):
    qi = pl.program_id(2)
    lane = lax.broadcasted_iota(jnp.int32, (tq, LANES), 1)
    q2 = (q_ref[0].astype(F32) * (HEAD_DIM ** -0.5 * LOG2E)).astype(BF16)
    zero = jnp.zeros_like(q2)
    qh = (jnp.where(lane < HEAD_DIM, q2, zero), jnp.where(lane >= HEAD_DIM, q2, zero))
    tri = tri_ref[...]

    n_row = tq // sub
    rows = lambda a: slice(a * sub, (a + 1) * sub)

    def start(h, a, kblk, mask, tri_b):
        z = lax.dot_general(qh[h][rows(a)], kblk, NT, preferred_element_type=F32)
        if mask is not None:
            z = jnp.where(mask, z, -MASKED)
        sp = jnp.maximum(z, 0.0) + jnp.log2(1.0 + jnp.exp2(-jnp.abs(z)))
        return z, jnp.dot(sp.astype(BF16), tri_b, preferred_element_type=F32)

    def finish(h, a, zc, carry, vblk):
        z, cs = zc
        c = cs + carry
        acc_ref[h, rows(a), :] += jnp.dot(jnp.exp2(z - c).astype(BF16), vblk, preferred_element_type=F32)
        return c[:, 0:1]

    def run(tiles, carries):
        carries = dict(carries)
        pending = []
        for (h, a, kblk, vblk, mask, tri_b) in tiles:
            pending.append((h, a, start(h, a, kblk, mask, tri_b), vblk))
            if len(pending) > TILES_AHEAD:
                ph, pa, pzc, pv = pending.pop(0)
                carries[ph, pa] = finish(ph, pa, pzc, carries[ph, pa], pv)
        for ph, pa, pzc, pv in pending:
            carries[ph, pa] = finish(ph, pa, pzc, carries[ph, pa], pv)
        return carries

    def min_carry(carries):
        m = None
        for c in carries.values():
            m = c if m is None else jnp.minimum(m, c)
        return jnp.min(m)

    acc_ref[...] = jnp.zeros_like(acc_ref)
    keys_at = lambda s0: k_ref[0, pl.ds(pl.multiple_of(s0, sub), sub), :]
    vals_at = lambda s0: v_ref[0, pl.ds(pl.multiple_of(s0, sub), sub), :]
    ri = lax.broadcasted_iota(jnp.int32, (sub, sub), 0)
    ci = lax.broadcasted_iota(jnp.int32, (sub, sub), 1)
    causal = ci < ri
    d0 = qi * tq
    heads_rows = [(h, a) for a in range(n_row) for h in range(2)]

    tiles = [(h, a, keys_at(d0 + a * sub), vals_at(d0 + a * sub), causal, tri) for (h, a) in heads_rows]
    for (h, a) in heads_rows:
        if a == 0:
            s0 = jnp.maximum(d0 - sub, 0)
            tiles.append((h, a, keys_at(s0), vals_at(s0), jnp.broadcast_to(qi > 0, (sub, sub)), tri))
        else:
            s0 = d0 + (a - 1) * sub
            tiles.append((h, a, keys_at(s0), vals_at(s0), None, tri))
    carries = run(tiles, {ha: jnp.zeros((sub, 1), F32) for ha in heads_rows})

    first = qi * n_row
    n_walk = jnp.maximum(first + n_row - 2, 0)

    def more(state):
        return jnp.logical_and(state[0] < n_walk, state[1] < CARRY_STOP)

    def body(state):
        i = state[0]
        carries = dict(zip(heads_rows, state[2:]))
        kb = n_walk - 1 - i
        kblk, vblk = keys_at(kb * sub), vals_at(kb * sub)
        tiles = []
        for (h, a) in heads_rows:
            fresh = None if a == n_row - 1 else jnp.broadcast_to(kb < first + a - 1, (sub, sub))
            tiles.append((h, a, kblk, vblk, fresh, tri))
        carries = run(tiles, carries)
        return (i + 1, min_carry(carries)) + tuple(carries[ha] for ha in heads_rows)

    state = (jnp.int32(0), min_carry(carries)) + tuple(carries[ha] for ha in heads_rows)
    state = lax.while_loop(more, body, state)
    carries = dict(zip(heads_rows, state[2:]))

    @pl.when(state[1] < CARRY_STOP)
    def _():
        meta_mask = lax.broadcasted_iota(jnp.int32, (sub, LANES), 1) < n_meta
        tri_m = tri_ref[0:LANES, 0:LANES]
        run([(h, a, km_ref[...], vm_ref[...], meta_mask, tri_m) for (h, a) in heads_rows], carries)

    o_ref[0] = jnp.where(lane < HEAD_DIM, acc_ref[0], acc_ref[1]).astype(o_ref.dtype)


def _sb_attn(qkv, qkv_meta, n_meta, tq, sub):
    b, s, w3 = qkv.shape
    width = w3 // 3
    pairs = width // LANES
    assert s % tq == 0 and tq % sub == 0 and sub % LANES == 0
    tri = (jnp.arange(sub)[:, None] >= jnp.arange(sub)[None, :]).astype(BF16)
    return pl.pallas_call(
        functools.partial(_sb_kernel, tq=tq, sub=sub, n_meta=n_meta),
        out_shape=jax.ShapeDtypeStruct((b, s, width), BF16),
        grid=(b, pairs, s // tq),
        in_specs=[
            pl.BlockSpec((1, tq, LANES), lambda bi, p, qi: (bi, qi, p)),
            pl.BlockSpec((1, s, LANES), lambda bi, p, qi: (bi, 0, pairs + p)),
            pl.BlockSpec((1, s, LANES), lambda bi, p, qi: (bi, 0, 2 * pairs + p)),
            pl.BlockSpec((LANES, LANES), lambda bi, p, qi: (0, pairs + p)),
            pl.BlockSpec((LANES, LANES), lambda bi, p, qi: (0, 2 * pairs + p)),
            pl.BlockSpec((sub, sub), lambda bi, p, qi: (0, 0)),
        ],
        out_specs=pl.BlockSpec((1, tq, LANES), lambda bi, p, qi: (bi, qi, p)),
        scratch_shapes=[pltpu.VMEM((2, tq, LANES), F32)],
        compiler_params=pltpu.CompilerParams(
            dimension_semantics=("parallel", "parallel", "arbitrary"),
            vmem_limit_bytes=VMEM_LIMIT),
        name="sb_attn",
    )(qkv, qkv, qkv, qkv_meta, qkv_meta, tri)


P_LORA = 1
P_CUMSUM = 2
P_MEAN = 2
P_SUM = 1
GROUP_PAIRS = 4
STAGE_DELAY = 0


def _rwkv_kernel(ux_ref, um_ref, mu_ref, wlo_ref, dbase_ref, abase_ref, gup_ref, kk_ref, ka_ref,
                 rk_ref, lng_ref, lnb_ref, bd_ref, tril_ref, o_ref, st_ref, carry_ref, y_ref,
                 *, width):
    c = pl.program_id(1)
    R, C = RW_ROWS, CHUNK
    n_chunks = R // C
    gw = GROUP_PAIRS * LANES
    n_groups = width // gw

    @pl.when(c == 0)
    def _():
        st_ref[...] = jnp.zeros_like(st_ref)
        carry_ref[...] = jnp.zeros_like(carry_ref)

    u = jnp.where(c == 0, um_ref[...], ux_ref[0])
    rolled = pltpu.roll(u, 1, axis=0)
    row0 = lax.broadcasted_iota(jnp.int32, (SUBLANES, u.shape[1]), 0) == 0
    prev = jnp.concatenate([jnp.where(row0, carry_ref[...], rolled[0:SUBLANES]), rolled[SUBLANES:]], axis=0)
    carry_ref[...] = u[R - 1:R, :]
    us = u + mu_ref[...] * (prev - u)

    xwa = us[:, 3 * width:3 * width + LANES]
    xg = us[:, 3 * width + LANES:3 * width + 2 * LANES]
    lane_wa = lax.broadcasted_iota(jnp.int32, xwa.shape, 1)
    twa = jnp.where(lane_wa < HEAD_DIM, jnp.tanh(xwa), xwa)
    sig_g = jax.nn.sigmoid(xg)
    bd = bd_ref[...]
    tril = tril_ref[...]

    ri = lax.broadcasted_iota(jnp.int32, (R, R), 0)
    ci = lax.broadcasted_iota(jnp.int32, (R, R), 1)
    same = (ri // C) == (ci // C)
    strict = same & (ci < ri)
    incl = same & (ci <= ri)
    eye = ci == ri
    head0 = (lax.broadcasted_iota(jnp.int32, (R, LANES), 1) < HEAD_DIM)
    head0_2 = jnp.concatenate([head0, head0], axis=1)
    hmask = (head0, ~head0)
    si = lax.broadcasted_iota(jnp.int32, (LANES, LANES), 0)
    sj = lax.broadcasted_iota(jnp.int32, (LANES, LANES), 1)
    st_same = (si // HEAD_DIM) == (sj // HEAD_DIM)
    st_eye = si == sj
    zero_b = jnp.zeros((R, LANES), BF16)

    def packed(x_bd):
        out = x_bd[0:C]
        for j in range(1, n_chunks):
            out = out + x_bd[j * C:(j + 1) * C]
        return out

    def block_diag(x_p):
        return jnp.where(same, jnp.concatenate([x_p.astype(BF16)] * n_chunks, axis=0), jnp.zeros((R, R), BF16))

    eye_p = packed(jnp.where(eye, 1.0, 0.0))

    def stages(g):
        base = g * gw
        cols = slice(base, base + gw)
        at = lambda off: slice(off + base, off + base + gw)
        heads = [(q, h) for q in range(GROUP_PAIRS) for h in range(2)]
        sls = [slice(q * LANES, (q + 1) * LANES) for q in range(GROUP_PAIRS)]

        rr, rk, rv = us[:, at(0)], us[:, at(width)], us[:, at(2 * width)]
        dec_pre = dbase_ref[:, cols] + _mm(twa, wlo_ref[:, at(0)], pa=P_LORA, pb=P_LORA)
        ld = -DECAY_SCALE * jax.nn.sigmoid(dec_pre)
        a = jax.nn.sigmoid(abase_ref[:, cols] + _mm(twa, wlo_ref[:, at(width)], pa=P_LORA, pb=P_LORA))
        gate = _mm(sig_g, gup_ref[:, cols], pa=P_LORA, pb=P_LORA)
        kkr = rk * kk_ref[:, cols]
        ss = _mm(kkr * kkr, bd, pa=P_SUM)
        kk = kkr * lax.rsqrt(jnp.maximum(ss, KK_EPS * KK_EPS))
        k = rk * (1.0 + (a - 1.0) * ka_ref[:, cols])
        beta = kk * a
        yield

        cs = _mm(tril, ld, pb=P_CUMSUM)
        d_last = [jnp.exp(cs[(j + 1) * C - 1:(j + 1) * C, :]) for j in range(n_chunks)]
        e_in = jnp.exp(cs)
        e_neg = 1.0 / e_in
        e_last = jnp.concatenate([jnp.broadcast_to(d, (C, gw)) for d in d_last], axis=0) * e_neg
        a_t = (-kk * jnp.exp(cs - ld)).astype(BF16)
        r_t = rr * e_in
        r_b = r_t.astype(BF16)
        b_t = (beta * e_neg).astype(BF16)
        k_t = (k * e_neg).astype(BF16)
        b_h = (beta * e_last).astype(BF16)
        k_h = (k * e_last).astype(BF16)
        v_b = rv.astype(BF16)
        yield

        l_ab, l_ak, p_rb, p_rk = {}, {}, {}, {}
        for (q, h) in heads:
            Ah = jnp.where(hmask[h], a_t[:, sls[q]], zero_b)
            Rh = jnp.where(hmask[h], r_b[:, sls[q]], zero_b)
            l_ab[q, h] = jnp.where(strict, _bdot(Ah, b_t[:, sls[q]], NT), 0.0)
            l_ak[q, h] = jnp.where(strict, _bdot(Ah, k_t[:, sls[q]], NT), 0.0).astype(BF16)
            p_rb[q, h] = jnp.where(incl, _bdot(Rh, b_t[:, sls[q]], NT), 0.0).astype(BF16)
            p_rk[q, h] = jnp.where(incl, _bdot(Rh, k_t[:, sls[q]], NT), 0.0).astype(BF16)
        yield

        pw_bd = {qh: l_ab[qh].astype(BF16) for qh in heads}
        pw = {qh: packed(l_ab[qh]) for qh in heads}
        ti = {qh: eye_p + pw[qh] for qh in heads}
        for qh in heads:
            pw[qh] = _bdot(pw[qh], pw_bd[qh])
        yield
        n_levels = 5
        for lvl in range(n_levels):
            last = lvl == n_levels - 1
            for qh in heads:
                pw_bd[qh] = block_diag(pw[qh])
                if last:
                    ti[qh] = ti[qh] + _bdot(ti[qh], pw_bd[qh])
                else:
                    both = _bdot(jnp.concatenate([ti[qh], pw[qh]], axis=0), pw_bd[qh])
                    ti[qh] = ti[qh] + both[0:C]
                    pw[qh] = both[C:2 * C]
            yield

        t_inv = {qh: block_diag(ti[qh]) for qh in heads}
        lv = {qh: _bdot(l_ak[qh], v_b[:, sls[qh[0]]]) for qh in heads}
        wu, ya, yb = {}, {}, {}
        for q in range(GROUP_PAIRS):
            lv2 = jnp.where(head0, lv[q, 0], lv[q, 1]).astype(BF16)
            rhs = jnp.concatenate([a_t[:, sls[q]], lv2], axis=1)
            wu[q] = jnp.where(head0_2, _bdot(t_inv[q, 0], rhs), _bdot(t_inv[q, 1], rhs)).astype(BF16)
        yield
        for q in range(GROUP_PAIRS):
            pwu = jnp.where(head0_2, _bdot(p_rb[q, 0], wu[q]), _bdot(p_rb[q, 1], wu[q]))
            pv = jnp.where(head0, _bdot(p_rk[q, 0], v_b[:, sls[q]]), _bdot(p_rk[q, 1], v_b[:, sls[q]]))
            ya[q] = (r_t[:, sls[q]] + pwu[:, :LANES]).astype(BF16)
            yb[q] = pwu[:, LANES:] + pv
        yield

        gm, hm_ = {}, {}
        for j in range(n_chunks):
            rs = slice(j * C, (j + 1) * C)
            for q in range(GROUP_PAIRS):
                gh = _bdot(b_h[rs, sls[q]], wu[q][rs], TN)
                kv = _bdot(k_h[rs, sls[q]], v_b[rs, sls[q]], TN)
                dl = jnp.broadcast_to(d_last[j][:, sls[q]], (LANES, LANES))
                gm[q, j] = (jnp.where(st_same, gh[:, :LANES], 0.0) + jnp.where(st_eye, dl, 0.0)).astype(BF16)
                hm_[q, j] = jnp.where(st_same, gh[:, LANES:] + kv, 0.0)
        yield

        m = {q: st_ref[g * GROUP_PAIRS + q] for q in range(GROUP_PAIRS)}
        for j in range(n_chunks):
            rs = slice(j * C, (j + 1) * C)
            for q in range(GROUP_PAIRS):
                mb = m[q].astype(BF16)
                y_ref[rs, base + q * LANES:base + (q + 1) * LANES] = _bdot(ya[q][rs], mb) + yb[q][rs]
                m[q] = _bdot(gm[q, j], mb) + hm_[q, j]
        for q in range(GROUP_PAIRS):
            st_ref[g * GROUP_PAIRS + q] = m[q]
        yield

        y = y_ref[:, cols]
        inv_n = 1.0 / HEAD_DIM
        mean = _mm(y, bd, pa=P_MEAN) * inv_n
        yc = y - mean
        var = _mm(yc * yc, bd, pa=P_SUM) * inv_n
        yn = yc * lax.rsqrt(var + GN_EPS) * lng_ref[:, cols] + lnb_ref[:, cols]
        bonus = _mm(rr * k * rk_ref[:, cols], bd, pa=1) * rv
        o_ref[0, :, cols] = ((yn + bonus) * gate).astype(o_ref.dtype)

    gens = [stages(g) for g in range(n_groups)]
    live = [True] * n_groups
    tick = 0
    while any(live):
        for g in range(n_groups):
            if live[g] and tick >= g * STAGE_DELAY:
                live[g] = next(gens[g], "done") != "done"
        tick += 1


def _rwkv(u_x, u_meta, mu, w_lora, dbase, abase, gate_up, k_k, k_a, r_k, ln_g, ln_b):
    b, s, uw = u_x.shape
    width = dbase.shape[1]
    assert s % RW_ROWS == 0 and uw == 3 * width + 2 * LANES and width % LANES == 0
    nc = s // RW_ROWS + 1
    gw = GROUP_PAIRS * LANES
    assert width % gw == 0
    hid = jnp.arange(gw) // HEAD_DIM
    bd = (hid[:, None] == hid[None, :]).astype(BF16)
    t = jnp.arange(RW_ROWS)
    tril = ((t[:, None] // CHUNK == t[None, :] // CHUNK) & (t[:, None] >= t[None, :])).astype(BF16)
    const = lambda shape: pl.BlockSpec(shape, lambda bi, c: tuple(0 for _ in shape))
    return pl.pallas_call(
        functools.partial(_rwkv_kernel, width=width),
        out_shape=jax.ShapeDtypeStruct((b, s, width), BF16),
        grid=(b, nc),
        in_specs=[
            pl.BlockSpec((1, RW_ROWS, uw), lambda bi, c: (bi, jnp.maximum(c - 1, 0), 0)),
            const((RW_ROWS, uw)), const((1, uw)), const((LANES, 2 * width)), const((1, width)),
            const((1, width)), const((LANES, width)), const((1, width)), const((1, width)),
            const((1, width)), const((1, width)), const((1, width)), const((gw, gw)),
            const((RW_ROWS, RW_ROWS)),
        ],
        out_specs=pl.BlockSpec((1, RW_ROWS, width), lambda bi, c: (bi, jnp.maximum(c - 1, 0), 0)),
        scratch_shapes=[
            pltpu.VMEM((width // LANES, LANES, LANES), F32),
            pltpu.VMEM((1, uw), F32),
            pltpu.VMEM((RW_ROWS, width), F32),
        ],
        compiler_params=pltpu.CompilerParams(
            dimension_semantics=("parallel", "arbitrary"), vmem_limit_bytes=VMEM_LIMIT),
        name="rwkv",
    )(u_x, u_meta, mu, w_lora, dbase, abase, gate_up, k_k, k_a, r_k, ln_g, ln_b, bd, tril)


def _merge_ffn_kernel(x_ref, osb_ref, orw_ref, gsb_ref, grw_ref, wsb_ref, wrw_ref, wout_ref, gmix_ref,
                      gpre_ref, wg_ref, wu_ref, wd_ref, gpost_ref, o_ref, *, ff_chunks):
    a = jnp.dot(osb_ref[...], wsb_ref[...], preferred_element_type=F32)
    b = jnp.dot(orw_ref[...], wrw_ref[...], preferred_element_type=F32)
    merged = gsb_ref[...].astype(F32) * a + grw_ref[...].astype(F32) * b
    y = jnp.dot(merged.astype(BF16), wout_ref[...], preferred_element_type=F32)
    h1 = x_ref[...] + _rms(y, gmix_ref[...])
    xn = _rms(h1, gpre_ref[...]).astype(BF16)
    f = None
    for c0, c1 in ff_chunks:
        gate = jnp.dot(xn, wg_ref[:, c0:c1], preferred_element_type=F32)
        up = jnp.dot(xn, wu_ref[:, c0:c1], preferred_element_type=F32)
        act = (gate * jax.nn.sigmoid(gate) * up).astype(BF16)
        t = jnp.dot(act, wd_ref[c0:c1, :], preferred_element_type=F32)
        f = t if f is None else f + t
    o_ref[...] = h1 + _rms(f, gpost_ref[...])


def _merge_ffn(x2d, o_sb, o_rw, gates, w_sb, w_rw, w_out, g_mix, g_pre, w_gate, w_up, w_down, g_post, tm):
    m, d = x2d.shape
    width = o_sb.shape[1]
    ff = w_gate.shape[1]
    assert m % tm == 0 and ff % LANES == 0
    edges = list(range(0, ff, FF_CHUNK)) + [ff]
    ff_chunks = tuple(zip(edges[:-1], edges[1:]))
    row = lambda cols, jb=0: pl.BlockSpec((tm, cols), lambda i: (i, jb))
    return pl.pallas_call(
        functools.partial(_merge_ffn_kernel, ff_chunks=ff_chunks),
        out_shape=jax.ShapeDtypeStruct((m, d), F32),
        grid=(m // tm,),
        in_specs=[row(d), row(width), row(width), row(d, 0), row(d, 1),
                  _resident((width, d)), _resident((width, d)), _resident((d, d)), _resident((1, d)),
                  _resident((1, d)), _resident((d, ff)), _resident((d, ff)), _resident((ff, d)),
                  _resident((1, d))],
        out_specs=row(d),
        compiler_params=pltpu.CompilerParams(
            dimension_semantics=("parallel",), vmem_limit_bytes=VMEM_LIMIT),
        name="merge_ffn",
    )(x2d, o_sb, o_rw, gates, gates, w_sb, w_rw, w_out, g_mix, g_pre, w_gate, w_up, w_down, g_post)


def _pick(n, prefs):
    for t in prefs:
        if n % t == 0:
            return t
    return n


def kernel(x, meta_tokens, norm_mix_pre, norm_mix_post, w_in, rw_shift_mu, rw_decay_up, rw_decay_base,
           rw_aaa_up, rw_aaa_base, rw_gate_up, rw_k_k, rw_k_a, rw_r_k, rw_ln_gain, rw_ln_bias,
           w_branch_sb, w_branch_rw, w_out, norm_ffn_pre, norm_ffn_post, w_ffn_gate, w_ffn_up,
           w_ffn_down):
    b, s, d = x.shape
    n_meta = meta_tokens.shape[0]
    depth = w_in.shape[0]
    assert depth == 1, "meta rows are only carried through the mixer of a single layer"
    width = w_branch_sb.shape[1]
    dlora = rw_decay_up.shape[1]
    alora = rw_aaa_up.shape[1]
    glora = rw_gate_up.shape[1]
    assert dlora + alora == LANES and glora == LANES and n_meta <= RW_ROWS
    l = 0
    m = b * s
    x2d = x.reshape(m, d)

    c_sb, c_rw = 3 * width, 3 * width + 3 * width + 2 * LANES
    w_in_b = w_in[l].astype(BF16)
    g_pre = norm_mix_pre[l][None, :]

    qkv, u_rw, gates = _proj(x2d, g_pre, w_in_b, c_sb, c_rw, _pick(m, (1024, 512, 256, 128)), "proj")
    meta_pad = jnp.zeros((LANES, d), F32).at[:n_meta].set(meta_tokens.astype(F32))
    qkv_meta, rw_meta, _ = _proj(meta_pad, g_pre, w_in_b, c_sb, c_rw, LANES, "proj_meta")
    row_ok = (jnp.arange(LANES) < n_meta)[:, None]
    qkv_meta = jnp.where(row_ok, qkv_meta, jnp.zeros_like(qkv_meta))
    u_meta = jnp.zeros((RW_ROWS, c_rw - c_sb), F32).at[RW_ROWS - n_meta:].set(rw_meta[:n_meta])

    tq = _pick(s, (1024, 512, 256, 128))
    o_sb = _sb_attn(qkv.reshape(b, s, c_sb), qkv_meta, n_meta, tq, min(tq, 256))

    w_lora = jnp.zeros((LANES, 2 * width), F32)
    w_lora = w_lora.at[:dlora, :width].set(rw_decay_up[l]).at[dlora:, width:].set(rw_aaa_up[l])
    vec = lambda p: p[l].reshape(1, -1).astype(F32)
    o_rw = _rwkv(u_rw.reshape(b, s, c_rw - c_sb), u_meta, vec(rw_shift_mu), w_lora,
                 vec(rw_decay_base), vec(rw_aaa_base), rw_gate_up[l].astype(F32), vec(rw_k_k),
                 vec(rw_k_a), vec(rw_r_k), vec(rw_ln_gain), vec(rw_ln_bias))

    out = _merge_ffn(x2d, o_sb.reshape(m, width), o_rw.reshape(m, width), gates,
                     w_branch_sb[l].astype(BF16), w_branch_rw[l].astype(BF16), w_out[l].astype(BF16),
                     norm_mix_post[l][None, :], norm_ffn_pre[l][None, :], w_ffn_gate[l].astype(BF16),
                     w_ffn_up[l].astype(BF16), w_ffn_down[l].astype(BF16), norm_ffn_post[l][None, :],
                     _pick(m, (512, 256, 128)))
    return out.reshape(b, s, d)
```

```python
import functools

import jax
import jax.numpy as jnp
from jax import lax
from jax.experimental import pallas as pl
from jax.experimental.pallas import tpu as pltpu

HEAD_DIM = 64
RMS_EPS = 1e-6
GN_EPS = 64e-5
KK_EPS = 1e-12
LANES = 128
SUBLANES = 8
DECAY_SCALE = 0.6065306597126334
CHUNK = 64
RW_ROWS = 256
RW_SEQS = 2
FF_CHUNK = 1024
VMEM_LIMIT = 56 * 1024 * 1024
LOG2E = 1.4426950408889634
MASKED = 1e30
CARRY_STOP = 64.0
SB_TILE = 256
NEAR = 128
TILES_AHEAD = 16

F32 = jnp.float32
BF16 = jnp.bfloat16

NN = (((1,), (0,)), ((), ()))
NT = (((1,), (1,)), ((), ()))
TN = (((0,), (0,)), ((), ()))


def _split(x, n):
    if x.dtype == BF16:
        return [x]
    parts = []
    r = x
    for i in range(n):
        p = r.astype(BF16)
        parts.append(p)
        if i + 1 < n:
            r = r - p.astype(F32)
    return parts


def _mm(a, b, dims=NN, pa=1, pb=1):
    a_parts = _split(a, pa)
    b_parts = _split(b, pb)
    order = max(len(a_parts), len(b_parts))
    acc = None
    for i, ai in enumerate(a_parts):
        for j, bj in enumerate(b_parts):
            if i + j >= order:
                continue
            t = lax.dot_general(ai, bj, dims, preferred_element_type=F32)
            acc = t if acc is None else acc + t
    return acc


def _bdot(a, b, dims=NN):
    return lax.dot_general(a.astype(BF16), b.astype(BF16), dims, preferred_element_type=F32)


def _rms(x, g):
    ms = jnp.mean(x * x, axis=-1, keepdims=True)
    return x * lax.rsqrt(ms + RMS_EPS) * g


def _resident(shape):
    return pl.BlockSpec(shape, lambda i: (0,) * len(shape), pipeline_mode=pl.Buffered(1))


def _proj_kernel(x_ref, g_ref, w_ref, qkv_ref, rw_ref, gate_ref, *, c_sb, c_rw):
    xn = _rms(x_ref[...], g_ref[...]).astype(BF16)
    qkv_ref[...] = jnp.dot(xn, w_ref[:, 0:c_sb], preferred_element_type=F32).astype(qkv_ref.dtype)
    rw_ref[...] = jnp.dot(xn, w_ref[:, c_sb:c_rw], preferred_element_type=F32)
    gate_ref[...] = jax.nn.sigmoid(
        jnp.dot(xn, w_ref[:, c_rw:], preferred_element_type=F32)).astype(gate_ref.dtype)


def _proj(x2d, gain, w, c_sb, c_rw, tm, name):
    m, d = x2d.shape
    n = w.shape[1]
    assert m % tm == 0
    row = lambda cols: pl.BlockSpec((tm, cols), lambda i: (i, 0))
    return pl.pallas_call(
        functools.partial(_proj_kernel, c_sb=c_sb, c_rw=c_rw),
        out_shape=(jax.ShapeDtypeStruct((m, c_sb), BF16),
                   jax.ShapeDtypeStruct((m, c_rw - c_sb), F32),
                   jax.ShapeDtypeStruct((m, n - c_rw), BF16)),
        grid=(m // tm,),
        in_specs=[row(d), _resident((1, d)), _resident((d, n))],
        out_specs=(row(c_sb), row(c_rw - c_sb), row(n - c_rw)),
        compiler_params=pltpu.CompilerParams(
            dimension_semantics=("parallel",), vmem_limit_bytes=VMEM_LIMIT),
        name=name,
    )(x2d, gain, w)


def _sb_kernel(q_ref, k_ref, v_ref, km_ref, vm_ref, tri_ref, o_ref, acc_ref, *, tq, sub, n_meta):
    qi = pl.program_id(2)
    lane = lax.broadcasted_iota(jnp.int32, (tq, LANES), 1)
    q2 = (q_ref[0].astype(F32) * (HEAD_DIM ** -0.5 * LOG2E)).astype(BF16)
    zero = jnp.zeros_like(q2)
    qh = (jnp.where(lane < HEAD_DIM, q2, zero), jnp.where(lane >= HEAD_DIM, q2, zero))
    tri = tri_ref[...]

    n_row = tq // sub
    rows = lambda a: slice(a * sub, (a + 1) * sub)

    def start(h, a, kblk, mask, tri_b):
        z = lax.dot_general(qh[h][rows(a)], kblk, NT, preferred_element_type=F32)
        if mask is not None:
            z = jnp.where(mask, z, -MASKED)
        sp = jnp.maximum(z, 0.0) + jnp.log2(1.0 + jnp.exp2(-jnp.abs(z)))
        return z, jnp.dot(sp.astype(BF16), tri_b, preferred_element_type=F32)

    def finish(h, a, zc, carry, vblk):
        z, cs = zc
        c = cs + carry
        acc_ref[h, rows(a), :] += jnp.dot(jnp.exp2(z - c).astype(BF16), vblk, preferred_element_type=F32)
        return c[:, 0:1]

    def run(tiles, carries):
        carries = dict(carries)
        pending = []
        for (h, a, kblk, vblk, mask, tri_b) in tiles:
            pending.append((h, a, start(h, a, kblk, mask, tri_b), vblk))
            if len(pending) > TILES_AHEAD:
                ph, pa, pzc, pv = pending.pop(0)
                carries[ph, pa] = finish(ph, pa, pzc, carries[ph, pa], pv)
        for ph, pa, pzc, pv in pending:
            carries[ph, pa] = finish(ph, pa, pzc, carries[ph, pa], pv)
        return carries

    def min_carry(carries):
        m = None
        for c in carries.values():
            m = c if m is None else jnp.minimum(m, c)
        return jnp.min(m)

    acc_ref[...] = jnp.zeros_like(acc_ref)
    keys_at = lambda s0: k_ref[0, pl.ds(pl.multiple_of(s0, sub), sub), :]
    vals_at = lambda s0: v_ref[0, pl.ds(pl.multiple_of(s0, sub), sub), :]
    ri = lax.broadcasted_iota(jnp.int32, (sub, sub), 0)
    ci = lax.broadcasted_iota(jnp.int32, (sub, sub), 1)
    causal = ci < ri
    d0 = qi * tq
    heads_rows = [(h, a) for a in range(n_row) for h in range(2)]

    near_mask = jnp.broadcast_to(qi > 0, (sub, NEAR))
    tri_n = tri_ref[0:NEAR, 0:NEAR]
    tiles = [(h, a, keys_at(d0 + a * sub), vals_at(d0 + a * sub), causal, tri) for (h, a) in heads_rows]
    for (h, a) in heads_rows:
        s0 = pl.multiple_of(jnp.maximum(d0 + a * sub - NEAR, 0), NEAR)
        tiles.append((h, a, k_ref[0, pl.ds(s0, NEAR), :], v_ref[0, pl.ds(s0, NEAR), :],
                      near_mask if a == 0 else None, tri_n))
    carries = run(tiles, {ha: jnp.zeros((sub, 1), F32) for ha in heads_rows})

    n_walk = qi * n_row + n_row - 1

    def more(state):
        return jnp.logical_and(state[0] < n_walk, state[1] < CARRY_STOP)

    def body(state):
        i = state[0]
        carries = dict(zip(heads_rows, state[2:]))
        kb = n_walk - 1 - i
        kblk, vblk = keys_at(kb * sub), vals_at(kb * sub)
        key_pos = ci + kb * sub
        tiles = [(h, a, kblk, vblk, key_pos < d0 + a * sub - NEAR, tri) for (h, a) in heads_rows]
        carries = run(tiles, carries)
        return (i + 1, min_carry(carries)) + tuple(carries[ha] for ha in heads_rows)

    state = (jnp.int32(0), min_carry(carries)) + tuple(carries[ha] for ha in heads_rows)
    state = lax.while_loop(more, body, state)
    carries = dict(zip(heads_rows, state[2:]))

    @pl.when(state[1] < CARRY_STOP)
    def _():
        meta_mask = lax.broadcasted_iota(jnp.int32, (sub, LANES), 1) < n_meta
        tri_m = tri_ref[0:LANES, 0:LANES]
        run([(h, a, km_ref[...], vm_ref[...], meta_mask, tri_m) for (h, a) in heads_rows], carries)

    o_ref[0] = jnp.where(lane < HEAD_DIM, acc_ref[0], acc_ref[1]).astype(o_ref.dtype)


def _sb_attn(qkv, qkv_meta, n_meta, tq, sub):
    b, s, w3 = qkv.shape
    width = w3 // 3
    pairs = width // LANES
    assert s % tq == 0 and tq % sub == 0 and sub % LANES == 0
    tri = (jnp.arange(sub)[:, None] >= jnp.arange(sub)[None, :]).astype(BF16)
    return pl.pallas_call(
        functools.partial(_sb_kernel, tq=tq, sub=sub, n_meta=n_meta),
        out_shape=jax.ShapeDtypeStruct((b, s, width), BF16),
        grid=(b, pairs, s // tq),
        in_specs=[
            pl.BlockSpec((1, tq, LANES), lambda bi, p, qi: (bi, qi, p)),
            pl.BlockSpec((1, s, LANES), lambda bi, p, qi: (bi, 0, pairs + p)),
            pl.BlockSpec((1, s, LANES), lambda bi, p, qi: (bi, 0, 2 * pairs + p)),
            pl.BlockSpec((LANES, LANES), lambda bi, p, qi: (0, pairs + p)),
            pl.BlockSpec((LANES, LANES), lambda bi, p, qi: (0, 2 * pairs + p)),
            pl.BlockSpec((sub, sub), lambda bi, p, qi: (0, 0)),
        ],
        out_specs=pl.BlockSpec((1, tq, LANES), lambda bi, p, qi: (bi, qi, p)),
        scratch_shapes=[pltpu.VMEM((2, tq, LANES), F32)],
        compiler_params=pltpu.CompilerParams(
            dimension_semantics=("parallel", "parallel", "arbitrary"),
            vmem_limit_bytes=VMEM_LIMIT),
        name="sb_attn",
    )(qkv, qkv, qkv, qkv_meta, qkv_meta, tri)


P_LORA = 1
P_CUMSUM = 2
P_MEAN = 2
P_SUM = 1
GROUP_PAIRS = 4
STAGE_DELAY = 0


def _rwkv_kernel(ux_ref, um_ref, mu_ref, wlo_ref, dbase_ref, abase_ref, gup_ref, kk_ref, ka_ref,
                 rk_ref, lng_ref, lnb_ref, bd_ref, tril_ref, o_ref, st_ref, carry_ref, y_ref,
                 *, width):
    c = pl.program_id(1)
    R, C = RW_ROWS, CHUNK
    n_chunks = R // C
    gw = GROUP_PAIRS * LANES
    n_groups = width // gw

    @pl.when(c == 0)
    def _():
        st_ref[...] = jnp.zeros_like(st_ref)
        carry_ref[...] = jnp.zeros_like(carry_ref)

    n_seq = ux_ref.shape[0]
    pairs = width // LANES
    row0 = lax.broadcasted_iota(jnp.int32, (SUBLANES, ux_ref.shape[2]), 0) == 0
    lane_wa = lax.broadcasted_iota(jnp.int32, (R, LANES), 1)
    shifted_cache = {}

    def shifted(bi):
        if bi not in shifted_cache:
            u = jnp.where(c == 0, um_ref[...], ux_ref[bi])
            rolled = pltpu.roll(u, 1, axis=0)
            prev = jnp.concatenate(
                [jnp.where(row0, carry_ref[bi], rolled[0:SUBLANES]), rolled[SUBLANES:]], axis=0)
            carry_ref[bi] = u[R - 1:R, :]
            us = u + mu_ref[...] * (prev - u)
            xwa = us[:, 3 * width:3 * width + LANES]
            xg = us[:, 3 * width + LANES:3 * width + 2 * LANES]
            twa = jnp.where(lane_wa < HEAD_DIM, jnp.tanh(xwa), xwa)
            shifted_cache[bi] = (us, twa, jax.nn.sigmoid(xg))
        return shifted_cache[bi]

    bd = bd_ref[...]
    tril = tril_ref[...]

    ri = lax.broadcasted_iota(jnp.int32, (R, R), 0)
    ci = lax.broadcasted_iota(jnp.int32, (R, R), 1)
    same = (ri // C) == (ci // C)
    strict = same & (ci < ri)
    incl = same & (ci <= ri)
    eye = ci == ri
    head0 = (lax.broadcasted_iota(jnp.int32, (R, LANES), 1) < HEAD_DIM)
    head0_2 = jnp.concatenate([head0, head0], axis=1)
    hmask = (head0, ~head0)
    si = lax.broadcasted_iota(jnp.int32, (LANES, LANES), 0)
    sj = lax.broadcasted_iota(jnp.int32, (LANES, LANES), 1)
    st_same = (si // HEAD_DIM) == (sj // HEAD_DIM)
    st_eye = si == sj
    zero_b = jnp.zeros((R, LANES), BF16)

    def packed(x_bd):
        out = x_bd[0:C]
        for j in range(1, n_chunks):
            out = out + x_bd[j * C:(j + 1) * C]
        return out

    def block_diag(x_p):
        return jnp.where(same, jnp.concatenate([x_p.astype(BF16)] * n_chunks, axis=0), jnp.zeros((R, R), BF16))

    eye_p = packed(jnp.where(eye, 1.0, 0.0))

    def stages(bi, g):
        base = g * gw
        cols = slice(base, base + gw)
        at = lambda off: slice(off + base, off + base + gw)
        heads = [(q, h) for q in range(GROUP_PAIRS) for h in range(2)]
        sls = [slice(q * LANES, (q + 1) * LANES) for q in range(GROUP_PAIRS)]
        us, twa, sig_g = shifted(bi)

        rr, rk, rv = us[:, at(0)], us[:, at(width)], us[:, at(2 * width)]
        dec_pre = dbase_ref[:, cols] + _mm(twa, wlo_ref[:, at(0)], pa=P_LORA, pb=P_LORA)
        ld = -DECAY_SCALE * jax.nn.sigmoid(dec_pre)
        a = jax.nn.sigmoid(abase_ref[:, cols] + _mm(twa, wlo_ref[:, at(width)], pa=P_LORA, pb=P_LORA))
        gate = _mm(sig_g, gup_ref[:, cols], pa=P_LORA, pb=P_LORA)
        kkr = rk * kk_ref[:, cols]
        ss = _mm(kkr * kkr, bd, pa=P_SUM)
        kk = kkr * lax.rsqrt(jnp.maximum(ss, KK_EPS * KK_EPS))
        k = rk * (1.0 + (a - 1.0) * ka_ref[:, cols])
        beta = kk * a
        yield

        cs = _mm(tril, ld, pb=P_CUMSUM)
        d_last = [jnp.exp(cs[(j + 1) * C - 1:(j + 1) * C, :]) for j in range(n_chunks)]
        e_in = jnp.exp(cs)
        e_neg = 1.0 / e_in
        e_last = jnp.concatenate([jnp.broadcast_to(d, (C, gw)) for d in d_last], axis=0) * e_neg
        a_t = (-kk * jnp.exp(cs - ld)).astype(BF16)
        r_t = rr * e_in
        r_b = r_t.astype(BF16)
        b_t = (beta * e_neg).astype(BF16)
        k_t = (k * e_neg).astype(BF16)
        b_h = (beta * e_last).astype(BF16)
        k_h = (k * e_last).astype(BF16)
        v_b = rv.astype(BF16)
        yield

        l_ab, l_ak, p_rb, p_rk = {}, {}, {}, {}
        for (q, h) in heads:
            Ah = jnp.where(hmask[h], a_t[:, sls[q]], zero_b)
            Rh = jnp.where(hmask[h], r_b[:, sls[q]], zero_b)
            l_ab[q, h] = jnp.where(strict, _bdot(Ah, b_t[:, sls[q]], NT), 0.0)
            l_ak[q, h] = jnp.where(strict, _bdot(Ah, k_t[:, sls[q]], NT), 0.0).astype(BF16)
            p_rb[q, h] = jnp.where(incl, _bdot(Rh, b_t[:, sls[q]], NT), 0.0).astype(BF16)
            p_rk[q, h] = jnp.where(incl, _bdot(Rh, k_t[:, sls[q]], NT), 0.0).astype(BF16)
        yield

        pw_bd = {qh: l_ab[qh].astype(BF16) for qh in heads}
        pw = {qh: packed(l_ab[qh]) for qh in heads}
        ti = {qh: eye_p + pw[qh] for qh in heads}
        for qh in heads:
            pw[qh] = _bdot(pw[qh], pw_bd[qh])
        yield
        n_levels = 5
        for lvl in range(n_levels):
            last = lvl == n_levels - 1
            for qh in heads:
                pw_bd[qh] = block_diag(pw[qh])
                if last:
                    ti[qh] = ti[qh] + _bdot(ti[qh], pw_bd[qh])
                else:
                    both = _bdot(jnp.concatenate([ti[qh], pw[qh]], axis=0), pw_bd[qh])
                    ti[qh] = ti[qh] + both[0:C]
                    pw[qh] = both[C:2 * C]
            yield

        t_inv = {qh: block_diag(ti[qh]) for qh in heads}
        lv = {qh: _bdot(l_ak[qh], v_b[:, sls[qh[0]]]) for qh in heads}
        wu, ya, yb = {}, {}, {}
        for q in range(GROUP_PAIRS):
            lv2 = jnp.where(head0, lv[q, 0], lv[q, 1]).astype(BF16)
            rhs = jnp.concatenate([a_t[:, sls[q]], lv2], axis=1)
            wu[q] = jnp.where(head0_2, _bdot(t_inv[q, 0], rhs), _bdot(t_inv[q, 1], rhs)).astype(BF16)
        yield
        for q in range(GROUP_PAIRS):
            pwu = jnp.where(head0_2, _bdot(p_rb[q, 0], wu[q]), _bdot(p_rb[q, 1], wu[q]))
            pv = jnp.where(head0, _bdot(p_rk[q, 0], v_b[:, sls[q]]), _bdot(p_rk[q, 1], v_b[:, sls[q]]))
            ya[q] = (r_t[:, sls[q]] + pwu[:, :LANES]).astype(BF16)
            yb[q] = pwu[:, LANES:] + pv
        yield

        gm, hm_ = {}, {}
        for j in range(n_chunks):
            rs = slice(j * C, (j + 1) * C)
            for q in range(GROUP_PAIRS):
                gh = _bdot(b_h[rs, sls[q]], wu[q][rs], TN)
                kv = _bdot(k_h[rs, sls[q]], v_b[rs, sls[q]], TN)
                dl = jnp.broadcast_to(d_last[j][:, sls[q]], (LANES, LANES))
                gm[q, j] = (jnp.where(st_same, gh[:, :LANES], 0.0) + jnp.where(st_eye, dl, 0.0)).astype(BF16)
                hm_[q, j] = jnp.where(st_same, gh[:, LANES:] + kv, 0.0)
        yield

        slot = lambda q: bi * pairs + g * GROUP_PAIRS + q
        m = {q: st_ref[slot(q)] for q in range(GROUP_PAIRS)}
        for j in range(n_chunks):
            rs = slice(j * C, (j + 1) * C)
            for q in range(GROUP_PAIRS):
                mb = m[q].astype(BF16)
                y_ref[bi, rs, base + q * LANES:base + (q + 1) * LANES] = _bdot(ya[q][rs], mb) + yb[q][rs]
                m[q] = _bdot(gm[q, j], mb) + hm_[q, j]
        for q in range(GROUP_PAIRS):
            st_ref[slot(q)] = m[q]
        yield

        y = y_ref[bi, :, cols]
        inv_n = 1.0 / HEAD_DIM
        mean = _mm(y, bd, pa=P_MEAN) * inv_n
        yc = y - mean
        var = _mm(yc * yc, bd, pa=P_SUM) * inv_n
        yn = yc * lax.rsqrt(var + GN_EPS) * lng_ref[:, cols] + lnb_ref[:, cols]
        bonus = _mm(rr * k * rk_ref[:, cols], bd, pa=1) * rv
        o_ref[bi, :, cols] = ((yn + bonus) * gate).astype(o_ref.dtype)

    gens = [(g, stages(bi, g)) for g in range(n_groups) for bi in range(n_seq)]
    live = [True] * len(gens)
    tick = 0
    while any(live):
        for idx, (g, gen) in enumerate(gens):
            if live[idx] and tick >= g * STAGE_DELAY:
                live[idx] = next(gen, "done") != "done"
        tick += 1


def _rwkv(u_x, u_meta, mu, w_lora, dbase, abase, gate_up, k_k, k_a, r_k, ln_g, ln_b):
    b, s, uw = u_x.shape
    width = dbase.shape[1]
    assert s % RW_ROWS == 0 and uw == 3 * width + 2 * LANES and width % LANES == 0
    nc = s // RW_ROWS + 1
    gw = GROUP_PAIRS * LANES
    assert width % gw == 0
    hid = jnp.arange(gw) // HEAD_DIM
    bd = (hid[:, None] == hid[None, :]).astype(BF16)
    t = jnp.arange(RW_ROWS)
    tril = ((t[:, None] // CHUNK == t[None, :] // CHUNK) & (t[:, None] >= t[None, :])).astype(BF16)
    const = lambda shape: pl.BlockSpec(shape, lambda bi, c: tuple(0 for _ in shape))
    n_seq = RW_SEQS if b % RW_SEQS == 0 else 1
    return pl.pallas_call(
        functools.partial(_rwkv_kernel, width=width),
        out_shape=jax.ShapeDtypeStruct((b, s, width), BF16),
        grid=(b // n_seq, nc),
        in_specs=[
            pl.BlockSpec((n_seq, RW_ROWS, uw), lambda bi, c: (bi, jnp.maximum(c - 1, 0), 0)),
            const((RW_ROWS, uw)), const((1, uw)), const((LANES, 2 * width)), const((1, width)),
            const((1, width)), const((LANES, width)), const((1, width)), const((1, width)),
            const((1, width)), const((1, width)), const((1, width)), const((gw, gw)),
            const((RW_ROWS, RW_ROWS)),
        ],
        out_specs=pl.BlockSpec((n_seq, RW_ROWS, width), lambda bi, c: (bi, jnp.maximum(c - 1, 0), 0)),
        scratch_shapes=[
            pltpu.VMEM((n_seq * (width // LANES), LANES, LANES), F32),
            pltpu.VMEM((n_seq, 1, uw), F32),
            pltpu.VMEM((n_seq, RW_ROWS, width), F32),
        ],
        compiler_params=pltpu.CompilerParams(
            dimension_semantics=("parallel", "arbitrary"), vmem_limit_bytes=VMEM_LIMIT),
        name="rwkv",
    )(u_x, u_meta, mu, w_lora, dbase, abase, gate_up, k_k, k_a, r_k, ln_g, ln_b, bd, tril)


def _merge_ffn_kernel(x_ref, osb_ref, orw_ref, gsb_ref, grw_ref, wsb_ref, wrw_ref, wout_ref, gmix_ref,
                      gpre_ref, wg_ref, wu_ref, wd_ref, gpost_ref, o_ref, *, ff_chunks):
    a = jnp.dot(osb_ref[...], wsb_ref[...], preferred_element_type=F32)
    b = jnp.dot(orw_ref[...], wrw_ref[...], preferred_element_type=F32)
    merged = gsb_ref[...].astype(F32) * a + grw_ref[...].astype(F32) * b
    y = jnp.dot(merged.astype(BF16), wout_ref[...], preferred_element_type=F32)
    h1 = x_ref[...] + _rms(y, gmix_ref[...])
    xn = _rms(h1, gpre_ref[...]).astype(BF16)
    f = None
    for c0, c1 in ff_chunks:
        gate = jnp.dot(xn, wg_ref[:, c0:c1], preferred_element_type=F32)
        up = jnp.dot(xn, wu_ref[:, c0:c1], preferred_element_type=F32)
        act = (gate * jax.nn.sigmoid(gate) * up).astype(BF16)
        t = jnp.dot(act, wd_ref[c0:c1, :], preferred_element_type=F32)
        f = t if f is None else f + t
    o_ref[...] = h1 + _rms(f, gpost_ref[...])


def _merge_ffn(x2d, o_sb, o_rw, gates, w_sb, w_rw, w_out, g_mix, g_pre, w_gate, w_up, w_down, g_post, tm):
    m, d = x2d.shape
    width = o_sb.shape[1]
    ff = w_gate.shape[1]
    assert m % tm == 0 and ff % LANES == 0
    edges = list(range(0, ff, FF_CHUNK)) + [ff]
    ff_chunks = tuple(zip(edges[:-1], edges[1:]))
    row = lambda cols, jb=0: pl.BlockSpec((tm, cols), lambda i: (i, jb))
    return pl.pallas_call(
        functools.partial(_merge_ffn_kernel, ff_chunks=ff_chunks),
        out_shape=jax.ShapeDtypeStruct((m, d), F32),
        grid=(m // tm,),
        in_specs=[row(d), row(width), row(width), row(d, 0), row(d, 1),
                  _resident((width, d)), _resident((width, d)), _resident((d, d)), _resident((1, d)),
                  _resident((1, d)), _resident((d, ff)), _resident((d, ff)), _resident((ff, d)),
                  _resident((1, d))],
        out_specs=row(d),
        compiler_params=pltpu.CompilerParams(
            dimension_semantics=("parallel",), vmem_limit_bytes=VMEM_LIMIT),
        name="merge_ffn",
    )(x2d, o_sb, o_rw, gates, gates, w_sb, w_rw, w_out, g_mix, g_pre, w_gate, w_up, w_down, g_post)


def _pick(n, prefs):
    for t in prefs:
        if n % t == 0:
            return t
    return n


def kernel(x, meta_tokens, norm_mix_pre, norm_mix_post, w_in, rw_shift_mu, rw_decay_up, rw_decay_base,
           rw_aaa_up, rw_aaa_base, rw_gate_up, rw_k_k, rw_k_a, rw_r_k, rw_ln_gain, rw_ln_bias,
           w_branch_sb, w_branch_rw, w_out, norm_ffn_pre, norm_ffn_post, w_ffn_gate, w_ffn_up,
           w_ffn_down):
    b, s, d = x.shape
    n_meta = meta_tokens.shape[0]
    depth = w_in.shape[0]
    assert depth == 1, "meta rows are only carried through the mixer of a single layer"
    width = w_branch_sb.shape[1]
    dlora = rw_decay_up.shape[1]
    alora = rw_aaa_up.shape[1]
    glora = rw_gate_up.shape[1]
    assert dlora + alora == LANES and glora == LANES and n_meta <= RW_ROWS
    l = 0
    m = b * s
    x2d = x.reshape(m, d)

    c_sb, c_rw = 3 * width, 3 * width + 3 * width + 2 * LANES
    w_in_b = w_in[l].astype(BF16)
    g_pre = norm_mix_pre[l][None, :]

    qkv, u_rw, gates = _proj(x2d, g_pre, w_in_b, c_sb, c_rw, _pick(m, (1024, 512, 256, 128)), "proj")
    meta_pad = jnp.zeros((LANES, d), F32).at[:n_meta].set(meta_tokens.astype(F32))
    qkv_meta, rw_meta, _ = _proj(meta_pad, g_pre, w_in_b, c_sb, c_rw, LANES, "proj_meta")
    row_ok = (jnp.arange(LANES) < n_meta)[:, None]
    qkv_meta = jnp.where(row_ok, qkv_meta, jnp.zeros_like(qkv_meta))
    u_meta = jnp.zeros((RW_ROWS, c_rw - c_sb), F32).at[RW_ROWS - n_meta:].set(rw_meta[:n_meta])

    tq = _pick(s, (1024, 512, 256, 128))
    o_sb = _sb_attn(qkv.reshape(b, s, c_sb), qkv_meta, n_meta, tq, min(tq, SB_TILE))

    w_lora = jnp.zeros((LANES, 2 * width), F32)
    w_lora = w_lora.at[:dlora, :width].set(rw_decay_up[l]).at[dlora:, width:].set(rw_aaa_up[l])
    vec = lambda p: p[l].reshape(1, -1).astype(F32)
    o_rw = _rwkv(u_rw.reshape(b, s, c_rw - c_sb), u_meta, vec(rw_shift_mu), w_lora,
                 vec(rw_decay_base), vec(rw_aaa_base), rw_gate_up[l].astype(F32), vec(rw_k_k),
                 vec(rw_k_a), vec(rw_r_k), vec(rw_ln_gain), vec(rw_ln_bias))

    out = _merge_ffn(x2d, o_sb.reshape(m, width), o_rw.reshape(m, width), gates,
                     w_branch_sb[l].astype(BF16), w_branch_rw[l].astype(BF16), w_out[l].astype(BF16),
                     norm_mix_post[l][None, :], norm_ffn_pre[l][None, :], w_ffn_gate[l].astype(BF16),
                     w_ffn_up[l].astype(BF16), w_ffn_down[l].astype(BF16), norm_ffn_post[l][None, :],
                     _pick(m, (512, 256, 128)))
    return out.reshape(b, s, d)
```

```python
import functools

import jax
import jax.numpy as jnp
from jax import lax
from jax.experimental import pallas as pl
from jax.experimental.pallas import tpu as pltpu

HEAD_DIM = 64
RMS_EPS = 1e-6
GN_EPS = 64e-5
KK_EPS = 1e-12
LANES = 128
SUBLANES = 8
DECAY_SCALE = 0.6065306597126334
CHUNK = 64
RW_ROWS = 256
RW_SEQS = 2
FF_CHUNK = 1024
VMEM_LIMIT = 56 * 1024 * 1024
LOG2E = 1.4426950408889634
MASKED = 1e30
CARRY_STOP = 64.0
SB_TILE = 256
NEAR = 128
TILES_AHEAD = 16

F32 = jnp.float32
BF16 = jnp.bfloat16

NN = (((1,), (0,)), ((), ()))
NT = (((1,), (1,)), ((), ()))
TN = (((0,), (0,)), ((), ()))


def _split(x, n):
    if x.dtype == BF16:
        return [x]
    parts = []
    r = x
    for i in range(n):
        p = r.astype(BF16)
        parts.append(p)
        if i + 1 < n:
            r = r - p.astype(F32)
    return parts


def _mm(a, b, dims=NN, pa=1, pb=1):
    a_parts = _split(a, pa)
    b_parts = _split(b, pb)
    order = max(len(a_parts), len(b_parts))
    acc = None
    for i, ai in enumerate(a_parts):
        for j, bj in enumerate(b_parts):
            if i + j >= order:
                continue
            t = lax.dot_general(ai, bj, dims, preferred_element_type=F32)
            acc = t if acc is None else acc + t
    return acc


def _bdot(a, b, dims=NN):
    return lax.dot_general(a.astype(BF16), b.astype(BF16), dims, preferred_element_type=F32)


def _rms(x, g):
    ms = jnp.mean(x * x, axis=-1, keepdims=True)
    return x * lax.rsqrt(ms + RMS_EPS) * g


def _resident(shape):
    return pl.BlockSpec(shape, lambda i: (0,) * len(shape), pipeline_mode=pl.Buffered(1))


def _proj_kernel(x_ref, g_ref, w_ref, qkv_ref, rw_ref, gate_ref, *, c_sb, c_rw):
    xn = _rms(x_ref[...], g_ref[...]).astype(BF16)
    qkv_ref[...] = jnp.dot(xn, w_ref[:, 0:c_sb], preferred_element_type=F32).astype(qkv_ref.dtype)
    rw_ref[...] = jnp.dot(xn, w_ref[:, c_sb:c_rw], preferred_element_type=F32)
    gate_ref[...] = jax.nn.sigmoid(
        jnp.dot(xn, w_ref[:, c_rw:], preferred_element_type=F32)).astype(gate_ref.dtype)


def _proj(x2d, gain, w, c_sb, c_rw, tm, name):
    m, d = x2d.shape
    n = w.shape[1]
    assert m % tm == 0
    row = lambda cols: pl.BlockSpec((tm, cols), lambda i: (i, 0))
    return pl.pallas_call(
        functools.partial(_proj_kernel, c_sb=c_sb, c_rw=c_rw),
        out_shape=(jax.ShapeDtypeStruct((m, c_sb), BF16),
                   jax.ShapeDtypeStruct((m, c_rw - c_sb), F32),
                   jax.ShapeDtypeStruct((m, n - c_rw), BF16)),
        grid=(m // tm,),
        in_specs=[row(d), _resident((1, d)), _resident((d, n))],
        out_specs=(row(c_sb), row(c_rw - c_sb), row(n - c_rw)),
        compiler_params=pltpu.CompilerParams(
            dimension_semantics=("parallel",), vmem_limit_bytes=VMEM_LIMIT),
        name=name,
    )(x2d, gain, w)


def _sb_kernel(q_ref, k_ref, v_ref, km_ref, vm_ref, tri_ref, o_ref, acc_ref, *, tq, sub, n_meta):
    qi = pl.program_id(2)
    lane = lax.broadcasted_iota(jnp.int32, (tq, LANES), 1)
    q2 = (q_ref[0].astype(F32) * (HEAD_DIM ** -0.5 * LOG2E)).astype(BF16)
    zero = jnp.zeros_like(q2)
    qh = (jnp.where(lane < HEAD_DIM, q2, zero), jnp.where(lane >= HEAD_DIM, q2, zero))
    tri = tri_ref[...]

    n_row = tq // sub
    rows = lambda a: slice(a * sub, (a + 1) * sub)

    def start(h, a, kblk, mask, tri_b):
        z = lax.dot_general(qh[h][rows(a)], kblk, NT, preferred_element_type=F32)
        if mask is not None:
            z = jnp.where(mask, z, -MASKED)
        sp = jnp.maximum(z, 0.0) + jnp.log2(1.0 + jnp.exp2(-jnp.abs(z)))
        return z, jnp.dot(sp.astype(BF16), tri_b, preferred_element_type=F32)

    def finish(h, a, zc, carry, vblk):
        z, cs = zc
        c = cs + carry
        acc_ref[h, rows(a), :] += jnp.dot(jnp.exp2(z - c).astype(BF16), vblk, preferred_element_type=F32)
        return c[:, 0:1]

    def run(tiles, carries):
        carries = dict(carries)
        pending = []
        for (h, a, kblk, vblk, mask, tri_b) in tiles:
            pending.append((h, a, start(h, a, kblk, mask, tri_b), vblk))
            if len(pending) > TILES_AHEAD:
                ph, pa, pzc, pv = pending.pop(0)
                carries[ph, pa] = finish(ph, pa, pzc, carries[ph, pa], pv)
        for ph, pa, pzc, pv in pending:
            carries[ph, pa] = finish(ph, pa, pzc, carries[ph, pa], pv)
        return carries

    def min_carry(carries):
        m = None
        for c in carries.values():
            m = c if m is None else jnp.minimum(m, c)
        return jnp.min(m)

    acc_ref[...] = jnp.zeros_like(acc_ref)
    keys_at = lambda s0: k_ref[0, pl.ds(pl.multiple_of(s0, sub), sub), :]
    vals_at = lambda s0: v_ref[0, pl.ds(pl.multiple_of(s0, sub), sub), :]
    ri = lax.broadcasted_iota(jnp.int32, (sub, sub), 0)
    ci = lax.broadcasted_iota(jnp.int32, (sub, sub), 1)
    causal = ci < ri
    d0 = qi * tq
    heads_rows = [(h, a) for a in range(n_row) for h in range(2)]

    near_mask = jnp.broadcast_to(qi > 0, (sub, NEAR))
    tri_n = tri_ref[0:NEAR, 0:NEAR]
    tiles = [(h, a, keys_at(d0 + a * sub), vals_at(d0 + a * sub), causal, tri) for (h, a) in heads_rows]
    for (h, a) in heads_rows:
        s0 = pl.multiple_of(jnp.maximum(d0 + a * sub - NEAR, 0), NEAR)
        tiles.append((h, a, k_ref[0, pl.ds(s0, NEAR), :], v_ref[0, pl.ds(s0, NEAR), :],
                      near_mask if a == 0 else None, tri_n))
    carries = run(tiles, {ha: jnp.zeros((sub, 1), F32) for ha in heads_rows})

    n_walk = qi * n_row + n_row - 1

    def more(state):
        return jnp.logical_and(state[0] < n_walk, state[1] < CARRY_STOP)

    def body(state):
        i = state[0]
        carries = dict(zip(heads_rows, state[2:]))
        kb = n_walk - 1 - i
        kblk, vblk = keys_at(kb * sub), vals_at(kb * sub)
        key_pos = ci + kb * sub
        tiles = [(h, a, kblk, vblk, key_pos < d0 + a * sub - NEAR, tri) for (h, a) in heads_rows]
        carries = run(tiles, carries)
        return (i + 1, min_carry(carries)) + tuple(carries[ha] for ha in heads_rows)

    state = (jnp.int32(0), min_carry(carries)) + tuple(carries[ha] for ha in heads_rows)
    state = lax.while_loop(more, body, state)
    carries = dict(zip(heads_rows, state[2:]))

    @pl.when(state[1] < CARRY_STOP)
    def _():
        meta_mask = lax.broadcasted_iota(jnp.int32, (sub, LANES), 1) < n_meta
        tri_m = tri_ref[0:LANES, 0:LANES]
        run([(h, a, km_ref[...], vm_ref[...], meta_mask, tri_m) for (h, a) in heads_rows], carries)

    o_ref[0] = jnp.where(lane < HEAD_DIM, acc_ref[0], acc_ref[1]).astype(o_ref.dtype)


def _sb_attn(qkv, qkv_meta, n_meta, tq, sub):
    b, s, w3 = qkv.shape
    width = w3 // 3
    pairs = width // LANES
    assert s % tq == 0 and tq % sub == 0 and sub % LANES == 0
    tri = (jnp.arange(sub)[:, None] >= jnp.arange(sub)[None, :]).astype(BF16)
    return pl.pallas_call(
        functools.partial(_sb_kernel, tq=tq, sub=sub, n_meta=n_meta),
        out_shape=jax.ShapeDtypeStruct((b, s, width), BF16),
        grid=(b, pairs, s // tq),
        in_specs=[
            pl.BlockSpec((1, tq, LANES), lambda bi, p, qi: (bi, qi, p)),
            pl.BlockSpec((1, s, LANES), lambda bi, p, qi: (bi, 0, pairs + p)),
            pl.BlockSpec((1, s, LANES), lambda bi, p, qi: (bi, 0, 2 * pairs + p)),
            pl.BlockSpec((LANES, LANES), lambda bi, p, qi: (0, pairs + p)),
            pl.BlockSpec((LANES, LANES), lambda bi, p, qi: (0, 2 * pairs + p)),
            pl.BlockSpec((sub, sub), lambda bi, p, qi: (0, 0)),
        ],
        out_specs=pl.BlockSpec((1, tq, LANES), lambda bi, p, qi: (bi, qi, p)),
        scratch_shapes=[pltpu.VMEM((2, tq, LANES), F32)],
        compiler_params=pltpu.CompilerParams(
            dimension_semantics=("parallel", "parallel", "arbitrary"),
            vmem_limit_bytes=VMEM_LIMIT),
        name="sb_attn",
    )(qkv, qkv, qkv, qkv_meta, qkv_meta, tri)


P_LORA = 1
P_CUMSUM = 2
P_MEAN = 2
P_SUM = 1


def _rwkv_kernel(ux_ref, um_ref, mu_ref, wlo_ref, dbase_ref, abase_ref, gup_ref, kk_ref, ka_ref,
                 rk_ref, lng_ref, lnb_ref, bd_ref, tril_ref, o_ref, st_ref, carry_ref, y_ref,
                 *, width):
    c = pl.program_id(1)
    R, C = RW_ROWS, CHUNK
    n_chunks = R // C

    @pl.when(c == 0)
    def _():
        st_ref[...] = jnp.zeros_like(st_ref)
        carry_ref[...] = jnp.zeros_like(carry_ref)

    n_seq = ux_ref.shape[0]
    pairs = width // LANES
    row0 = lax.broadcasted_iota(jnp.int32, (SUBLANES, ux_ref.shape[2]), 0) == 0
    lane_wa = lax.broadcasted_iota(jnp.int32, (R, LANES), 1)
    shifted_cache = {}

    def shifted(bi):
        if bi not in shifted_cache:
            u = jnp.where(c == 0, um_ref[...], ux_ref[bi])
            rolled = pltpu.roll(u, 1, axis=0)
            prev = jnp.concatenate(
                [jnp.where(row0, carry_ref[bi], rolled[0:SUBLANES]), rolled[SUBLANES:]], axis=0)
            carry_ref[bi] = u[R - 1:R, :]
            us = u + mu_ref[...] * (prev - u)
            xwa = us[:, 3 * width:3 * width + LANES]
            xg = us[:, 3 * width + LANES:3 * width + 2 * LANES]
            twa = jnp.where(lane_wa < HEAD_DIM, jnp.tanh(xwa), xwa)
            shifted_cache[bi] = (us, twa, jax.nn.sigmoid(xg))
        return shifted_cache[bi]

    bd = bd_ref[...]
    tril = tril_ref[...]

    ri = lax.broadcasted_iota(jnp.int32, (R, R), 0)
    ci = lax.broadcasted_iota(jnp.int32, (R, R), 1)
    same = (ri // C) == (ci // C)
    strict = same & (ci < ri)
    incl = same & (ci <= ri)
    eye = ci == ri
    head0 = (lax.broadcasted_iota(jnp.int32, (R, LANES), 1) < HEAD_DIM)
    head0_2 = jnp.concatenate([head0, head0], axis=1)
    hmask = (head0, ~head0)
    si = lax.broadcasted_iota(jnp.int32, (LANES, LANES), 0)
    sj = lax.broadcasted_iota(jnp.int32, (LANES, LANES), 1)
    st_same = (si // HEAD_DIM) == (sj // HEAD_DIM)
    st_eye = si == sj
    zero_b = jnp.zeros((R, LANES), BF16)

    def packed(x_bd):
        out = x_bd[0:C]
        for j in range(1, n_chunks):
            out = out + x_bd[j * C:(j + 1) * C]
        return out

    def block_diag(x_p):
        return jnp.where(same, jnp.concatenate([x_p.astype(BF16)] * n_chunks, axis=0), jnp.zeros((R, R), BF16))

    eye_p = packed(jnp.where(eye, 1.0, 0.0))

    def stages(bi):
        gw = width
        cols = slice(0, width)
        at = lambda off: slice(off, off + width)
        heads = [(q, h) for q in range(pairs) for h in range(2)]
        sls = [slice(q * LANES, (q + 1) * LANES) for q in range(pairs)]
        us, twa, sig_g = shifted(bi)

        rr, rk, rv = us[:, at(0)], us[:, at(width)], us[:, at(2 * width)]
        dec_pre = dbase_ref[:, cols] + _mm(twa, wlo_ref[:, at(0)], pa=P_LORA, pb=P_LORA)
        ld = -DECAY_SCALE * jax.nn.sigmoid(dec_pre)
        a = jax.nn.sigmoid(abase_ref[:, cols] + _mm(twa, wlo_ref[:, at(width)], pa=P_LORA, pb=P_LORA))
        gate = _mm(sig_g, gup_ref[:, cols], pa=P_LORA, pb=P_LORA)
        kkr = rk * kk_ref[:, cols]
        ss = _mm(kkr * kkr, bd, pa=P_SUM)
        kk = kkr * lax.rsqrt(jnp.maximum(ss, KK_EPS * KK_EPS))
        k = rk * (1.0 + (a - 1.0) * ka_ref[:, cols])
        beta = kk * a
        yield

        cs = _mm(tril, ld, pb=P_CUMSUM)
        d_last = [jnp.exp(cs[(j + 1) * C - 1:(j + 1) * C, :]) for j in range(n_chunks)]
        e_in = jnp.exp(cs)
        e_neg = 1.0 / e_in
        e_last = jnp.concatenate([jnp.broadcast_to(d, (C, gw)) for d in d_last], axis=0) * e_neg
        a_t = (-kk * jnp.exp(cs - ld)).astype(BF16)
        r_t = rr * e_in
        r_b = r_t.astype(BF16)
        b_t = (beta * e_neg).astype(BF16)
        k_t = (k * e_neg).astype(BF16)
        b_h = (beta * e_last).astype(BF16)
        k_h = (k * e_last).astype(BF16)
        v_b = rv.astype(BF16)
        yield

        l_ab, l_ak, p_rb, p_rk = {}, {}, {}, {}
        for (q, h) in heads:
            Ah = jnp.where(hmask[h], a_t[:, sls[q]], zero_b)
            Rh = jnp.where(hmask[h], r_b[:, sls[q]], zero_b)
            l_ab[q, h] = jnp.where(strict, _bdot(Ah, b_t[:, sls[q]], NT), 0.0)
            l_ak[q, h] = jnp.where(strict, _bdot(Ah, k_t[:, sls[q]], NT), 0.0).astype(BF16)
            p_rb[q, h] = jnp.where(incl, _bdot(Rh, b_t[:, sls[q]], NT), 0.0).astype(BF16)
            p_rk[q, h] = jnp.where(incl, _bdot(Rh, k_t[:, sls[q]], NT), 0.0).astype(BF16)
        yield

        pw_bd = {qh: l_ab[qh].astype(BF16) for qh in heads}
        pw = {qh: packed(l_ab[qh]) for qh in heads}
        ti = {qh: eye_p + pw[qh] for qh in heads}
        for qh in heads:
            pw[qh] = _bdot(pw[qh], pw_bd[qh])
        yield
        n_levels = 5
        for lvl in range(n_levels):
            last = lvl == n_levels - 1
            for qh in heads:
                pw_bd[qh] = block_diag(pw[qh])
                if last:
                    ti[qh] = ti[qh] + _bdot(ti[qh], pw_bd[qh])
                else:
                    both = _bdot(jnp.concatenate([ti[qh], pw[qh]], axis=0), pw_bd[qh])
                    ti[qh] = ti[qh] + both[0:C]
                    pw[qh] = both[C:2 * C]
            yield

        t_inv = {qh: block_diag(ti[qh]) for qh in heads}
        lv = {qh: _bdot(l_ak[qh], v_b[:, sls[qh[0]]]) for qh in heads}
        wu, ya, yb = {}, {}, {}
        for q in range(pairs):
            lv2 = jnp.where(head0, lv[q, 0], lv[q, 1]).astype(BF16)
            rhs = jnp.concatenate([a_t[:, sls[q]], lv2], axis=1)
            wu[q] = jnp.where(head0_2, _bdot(t_inv[q, 0], rhs), _bdot(t_inv[q, 1], rhs)).astype(BF16)
        yield
        for q in range(pairs):
            pwu = jnp.where(head0_2, _bdot(p_rb[q, 0], wu[q]), _bdot(p_rb[q, 1], wu[q]))
            pv = jnp.where(head0, _bdot(p_rk[q, 0], v_b[:, sls[q]]), _bdot(p_rk[q, 1], v_b[:, sls[q]]))
            ya[q] = (r_t[:, sls[q]] + pwu[:, :LANES]).astype(BF16)
            yb[q] = pwu[:, LANES:] + pv
        yield

        gm, hm_ = {}, {}
        for j in range(n_chunks):
            rs = slice(j * C, (j + 1) * C)
            for q in range(pairs):
                gh = _bdot(b_h[rs, sls[q]], wu[q][rs], TN)
                kv = _bdot(k_h[rs, sls[q]], v_b[rs, sls[q]], TN)
                dl = jnp.broadcast_to(d_last[j][:, sls[q]], (LANES, LANES))
                gm[q, j] = (jnp.where(st_same, gh[:, :LANES], 0.0) + jnp.where(st_eye, dl, 0.0)).astype(BF16)
                hm_[q, j] = jnp.where(st_same, gh[:, LANES:] + kv, 0.0)
        yield

        slot = lambda q: bi * pairs + q
        m = {q: st_ref[slot(q)] for q in range(pairs)}
        for j in range(n_chunks):
            rs = slice(j * C, (j + 1) * C)
            for q in range(pairs):
                mb = m[q].astype(BF16)
                y_ref[bi, rs, sls[q]] = _bdot(ya[q][rs], mb) + yb[q][rs]
                m[q] = _bdot(gm[q, j], mb) + hm_[q, j]
        for q in range(pairs):
            st_ref[slot(q)] = m[q]
        yield

        y = y_ref[bi, :, cols]
        inv_n = 1.0 / HEAD_DIM
        mean = _mm(y, bd, pa=P_MEAN) * inv_n
        yc = y - mean
        var = _mm(yc * yc, bd, pa=P_SUM) * inv_n
        yn = yc * lax.rsqrt(var + GN_EPS) * lng_ref[:, cols] + lnb_ref[:, cols]
        bonus = _mm(rr * k * rk_ref[:, cols], bd, pa=1) * rv
        o_ref[bi, :, cols] = ((yn + bonus) * gate).astype(o_ref.dtype)

    gens = [stages(bi) for bi in range(n_seq)]
    live = True
    while live:
        live = all([next(gen, "done") != "done" for gen in gens])


def _rwkv(u_x, u_meta, mu, w_lora, dbase, abase, gate_up, k_k, k_a, r_k, ln_g, ln_b):
    b, s, uw = u_x.shape
    width = dbase.shape[1]
    assert s % RW_ROWS == 0 and uw == 3 * width + 2 * LANES and width % LANES == 0
    nc = s // RW_ROWS + 1
    gw = width
    hid = jnp.arange(gw) // HEAD_DIM
    bd = (hid[:, None] == hid[None, :]).astype(BF16)
    t = jnp.arange(RW_ROWS)
    tril = ((t[:, None] // CHUNK == t[None, :] // CHUNK) & (t[:, None] >= t[None, :])).astype(BF16)
    const = lambda shape: pl.BlockSpec(shape, lambda bi, c: tuple(0 for _ in shape))
    n_seq = RW_SEQS if b % RW_SEQS == 0 else 1
    return pl.pallas_call(
        functools.partial(_rwkv_kernel, width=width),
        out_shape=jax.ShapeDtypeStruct((b, s, width), BF16),
        grid=(b // n_seq, nc),
        in_specs=[
            pl.BlockSpec((n_seq, RW_ROWS, uw), lambda bi, c: (bi, jnp.maximum(c - 1, 0), 0)),
            const((RW_ROWS, uw)), const((1, uw)), const((LANES, 2 * width)), const((1, width)),
            const((1, width)), const((LANES, width)), const((1, width)), const((1, width)),
            const((1, width)), const((1, width)), const((1, width)), const((gw, gw)),
            const((RW_ROWS, RW_ROWS)),
        ],
        out_specs=pl.BlockSpec((n_seq, RW_ROWS, width), lambda bi, c: (bi, jnp.maximum(c - 1, 0), 0)),
        scratch_shapes=[
            pltpu.VMEM((n_seq * (width // LANES), LANES, LANES), F32),
            pltpu.VMEM((n_seq, 1, uw), F32),
            pltpu.VMEM((n_seq, RW_ROWS, width), F32),
        ],
        compiler_params=pltpu.CompilerParams(
            dimension_semantics=("parallel", "arbitrary"), vmem_limit_bytes=VMEM_LIMIT),
        name="rwkv",
    )(u_x, u_meta, mu, w_lora, dbase, abase, gate_up, k_k, k_a, r_k, ln_g, ln_b, bd, tril)


def _merge_ffn_kernel(x_ref, osb_ref, orw_ref, gsb_ref, grw_ref, wsb_ref, wrw_ref, wout_ref, gmix_ref,
                      gpre_ref, wg_ref, wu_ref, wd_ref, gpost_ref, o_ref, *, ff_chunks):
    a = jnp.dot(osb_ref[...], wsb_ref[...], preferred_element_type=F32)
    b = jnp.dot(orw_ref[...], wrw_ref[...], preferred_element_type=F32)
    merged = gsb_ref[...].astype(F32) * a + grw_ref[...].astype(F32) * b
    y = jnp.dot(merged.astype(BF16), wout_ref[...], preferred_element_type=F32)
    h1 = x_ref[...] + _rms(y, gmix_ref[...])
    xn = _rms(h1, gpre_ref[...]).astype(BF16)
    f = None
    for c0, c1 in ff_chunks:
        gate = jnp.dot(xn, wg_ref[:, c0:c1], preferred_element_type=F32)
        up = jnp.dot(xn, wu_ref[:, c0:c1], preferred_element_type=F32)
        act = (gate * jax.nn.sigmoid(gate) * up).astype(BF16)
        t = jnp.dot(act, wd_ref[c0:c1, :], preferred_element_type=F32)
        f = t if f is None else f + t
    o_ref[...] = h1 + _rms(f, gpost_ref[...])


def _merge_ffn(x2d, o_sb, o_rw, gates, w_sb, w_rw, w_out, g_mix, g_pre, w_gate, w_up, w_down, g_post, tm):
    m, d = x2d.shape
    width = o_sb.shape[1]
    ff = w_gate.shape[1]
    assert m % tm == 0 and ff % LANES == 0
    edges = list(range(0, ff, FF_CHUNK)) + [ff]
    ff_chunks = tuple(zip(edges[:-1], edges[1:]))
    row = lambda cols, jb=0: pl.BlockSpec((tm, cols), lambda i: (i, jb))
    return pl.pallas_call(
        functools.partial(_merge_ffn_kernel, ff_chunks=ff_chunks),
        out_shape=jax.ShapeDtypeStruct((m, d), F32),
        grid=(m // tm,),
        in_specs=[row(d), row(width), row(width), row(d, 0), row(d, 1),
                  _resident((width, d)), _resident((width, d)), _resident((d, d)), _resident((1, d)),
                  _resident((1, d)), _resident((d, ff)), _resident((d, ff)), _resident((ff, d)),
                  _resident((1, d))],
        out_specs=row(d),
        compiler_params=pltpu.CompilerParams(
            dimension_semantics=("parallel",), vmem_limit_bytes=VMEM_LIMIT),
        name="merge_ffn",
    )(x2d, o_sb, o_rw, gates, gates, w_sb, w_rw, w_out, g_mix, g_pre, w_gate, w_up, w_down, g_post)


def _pick(n, prefs):
    for t in prefs:
        if n % t == 0:
            return t
    return n


def kernel(x, meta_tokens, norm_mix_pre, norm_mix_post, w_in, rw_shift_mu, rw_decay_up, rw_decay_base,
           rw_aaa_up, rw_aaa_base, rw_gate_up, rw_k_k, rw_k_a, rw_r_k, rw_ln_gain, rw_ln_bias,
           w_branch_sb, w_branch_rw, w_out, norm_ffn_pre, norm_ffn_post, w_ffn_gate, w_ffn_up,
           w_ffn_down):
    b, s, d = x.shape
    n_meta = meta_tokens.shape[0]
    depth = w_in.shape[0]
    assert depth == 1, "meta rows are only carried through the mixer of a single layer"
    width = w_branch_sb.shape[1]
    dlora = rw_decay_up.shape[1]
    alora = rw_aaa_up.shape[1]
    glora = rw_gate_up.shape[1]
    assert dlora + alora == LANES and glora == LANES and n_meta <= RW_ROWS
    l = 0
    m = b * s
    x2d = x.reshape(m, d)

    c_sb, c_rw = 3 * width, 3 * width + 3 * width + 2 * LANES
    w_in_b = w_in[l].astype(BF16)
    g_pre = norm_mix_pre[l][None, :]

    qkv, u_rw, gates = _proj(x2d, g_pre, w_in_b, c_sb, c_rw, _pick(m, (1024, 512, 256, 128)), "proj")
    meta_pad = jnp.zeros((LANES, d), F32).at[:n_meta].set(meta_tokens.astype(F32))
    qkv_meta, rw_meta, _ = _proj(meta_pad, g_pre, w_in_b, c_sb, c_rw, LANES, "proj_meta")
    row_ok = (jnp.arange(LANES) < n_meta)[:, None]
    qkv_meta = jnp.where(row_ok, qkv_meta, jnp.zeros_like(qkv_meta))
    u_meta = jnp.zeros((RW_ROWS, c_rw - c_sb), F32).at[RW_ROWS - n_meta:].set(rw_meta[:n_meta])

    tq = _pick(s, (1024, 512, 256, 128))
    o_sb = _sb_attn(qkv.reshape(b, s, c_sb), qkv_meta, n_meta, tq, min(tq, SB_TILE))

    w_lora = jnp.zeros((LANES, 2 * width), F32)
    w_lora = w_lora.at[:dlora, :width].set(rw_decay_up[l]).at[dlora:, width:].set(rw_aaa_up[l])
    vec = lambda p: p[l].reshape(1, -1).astype(F32)
    o_rw = _rwkv(u_rw.reshape(b, s, c_rw - c_sb), u_meta, vec(rw_shift_mu), w_lora,
                 vec(rw_decay_base), vec(rw_aaa_base), rw_gate_up[l].astype(F32), vec(rw_k_k),
                 vec(rw_k_a), vec(rw_r_k), vec(rw_ln_gain), vec(rw_ln_bias))

    out = _merge_ffn(x2d, o_sb.reshape(m, width), o_rw.reshape(m, width), gates,
                     w_branch_sb[l].astype(BF16), w_branch_rw[l].astype(BF16), w_out[l].astype(BF16),
                     norm_mix_post[l][None, :], norm_ffn_pre[l][None, :], w_ffn_gate[l].astype(BF16),
                     w_ffn_up[l].astype(BF16), w_ffn_down[l].astype(BF16), norm_ffn_post[l][None, :],
                     _pick(m, (512, 256, 128)))
    return out.reshape(b, s, d)
```

```python
import functools

import jax
import jax.numpy as jnp
from jax import lax
from jax.experimental import pallas as pl
from jax.experimental.pallas import tpu as pltpu

HEAD_DIM = 64
RMS_EPS = 1e-6
GN_EPS = 64e-5
KK_EPS = 1e-12
LANES = 128
SUBLANES = 8
DECAY_SCALE = 0.6065306597126334
CHUNK = 64
RW_ROWS = 256
RW_SEQS = 2
FF_CHUNK = 1024
VMEM_LIMIT = 56 * 1024 * 1024
LOG2E = 1.4426950408889634
MASKED = 1e30
CARRY_STOP = 64.0
SB_TILE = 256
NEAR = 128

F32 = jnp.float32
BF16 = jnp.bfloat16

NN = (((1,), (0,)), ((), ()))
NT = (((1,), (1,)), ((), ()))
TN = (((0,), (0,)), ((), ()))


def _split(x, n):
    if x.dtype == BF16:
        return [x]
    parts = []
    r = x
    for i in range(n):
        p = r.astype(BF16)
        parts.append(p)
        if i + 1 < n:
            r = r - p.astype(F32)
    return parts


def _mm(a, b, dims=NN, pa=1, pb=1):
    a_parts = _split(a, pa)
    b_parts = _split(b, pb)
    order = max(len(a_parts), len(b_parts))
    acc = None
    for i, ai in enumerate(a_parts):
        for j, bj in enumerate(b_parts):
            if i + j >= order:
                continue
            t = lax.dot_general(ai, bj, dims, preferred_element_type=F32)
            acc = t if acc is None else acc + t
    return acc


def _bdot(a, b, dims=NN):
    return lax.dot_general(a.astype(BF16), b.astype(BF16), dims, preferred_element_type=F32)


def _rms(x, g):
    ms = jnp.mean(x * x, axis=-1, keepdims=True)
    return x * lax.rsqrt(ms + RMS_EPS) * g


def _resident(shape):
    return pl.BlockSpec(shape, lambda i: (0,) * len(shape), pipeline_mode=pl.Buffered(1))


def _proj_kernel(x_ref, g_ref, w_ref, qkv_ref, rw_ref, gate_ref, *, c_sb, c_rw):
    xn = _rms(x_ref[...], g_ref[...]).astype(BF16)
    qkv_ref[...] = jnp.dot(xn, w_ref[:, 0:c_sb], preferred_element_type=F32).astype(qkv_ref.dtype)
    rw_ref[...] = jnp.dot(xn, w_ref[:, c_sb:c_rw], preferred_element_type=F32)
    gate_ref[...] = jax.nn.sigmoid(
        jnp.dot(xn, w_ref[:, c_rw:], preferred_element_type=F32)).astype(gate_ref.dtype)


def _proj(x2d, gain, w, c_sb, c_rw, tm, name):
    m, d = x2d.shape
    n = w.shape[1]
    assert m % tm == 0
    row = lambda cols: pl.BlockSpec((tm, cols), lambda i: (i, 0))
    return pl.pallas_call(
        functools.partial(_proj_kernel, c_sb=c_sb, c_rw=c_rw),
        out_shape=(jax.ShapeDtypeStruct((m, c_sb), BF16),
                   jax.ShapeDtypeStruct((m, c_rw - c_sb), F32),
                   jax.ShapeDtypeStruct((m, n - c_rw), BF16)),
        grid=(m // tm,),
        in_specs=[row(d), _resident((1, d)), _resident((d, n))],
        out_specs=(row(c_sb), row(c_rw - c_sb), row(n - c_rw)),
        compiler_params=pltpu.CompilerParams(
            dimension_semantics=("parallel",), vmem_limit_bytes=VMEM_LIMIT),
        name=name,
    )(x2d, gain, w)


def _sb_kernel(q_ref, k_ref, v_ref, km_ref, vm_ref, tri_ref, o_ref, acc_ref, *, tq, sub, n_meta):
    qi = pl.program_id(2)
    lane = lax.broadcasted_iota(jnp.int32, (tq, LANES), 1)
    q2 = (q_ref[0].astype(F32) * (HEAD_DIM ** -0.5 * LOG2E)).astype(BF16)
    zero = jnp.zeros_like(q2)
    qh = (jnp.where(lane < HEAD_DIM, q2, zero), jnp.where(lane >= HEAD_DIM, q2, zero))
    tri = tri_ref[...]

    n_row = tq // sub
    rows = lambda a: slice(a * sub, (a + 1) * sub)

    def start(h, a, kblk, mask, tri_b):
        z = lax.dot_general(qh[h][rows(a)], kblk, NT, preferred_element_type=F32)
        if mask is not None:
            z = jnp.where(mask, z, -MASKED)
        sp = jnp.maximum(z, 0.0) + jnp.log2(1.0 + jnp.exp2(-jnp.abs(z)))
        return z, jnp.dot(sp.astype(BF16), tri_b, preferred_element_type=F32)

    def finish(h, a, zc, carry, vblk):
        z, cs = zc
        c = cs + carry
        acc_ref[h, rows(a), :] += jnp.dot(jnp.exp2(z - c).astype(BF16), vblk, preferred_element_type=F32)

    def run(tiles, carries):
        carries = dict(carries)
        pending = []
        for (h, a, kblk, vblk, mask, tri_b) in tiles:
            zc = start(h, a, kblk, mask, tri_b)
            pending.append((h, a, zc, carries[h, a], vblk))
            carries[h, a] = zc[1][:, 0:1] + carries[h, a]
        lowest = None
        for c in carries.values():
            lowest = c if lowest is None else jnp.minimum(lowest, c)
        lowest = jnp.min(lowest)
        for item in pending:
            finish(*item)
        return carries, lowest

    acc_ref[...] = jnp.zeros_like(acc_ref)
    keys_at = lambda s0: k_ref[0, pl.ds(pl.multiple_of(s0, sub), sub), :]
    vals_at = lambda s0: v_ref[0, pl.ds(pl.multiple_of(s0, sub), sub), :]
    ri = lax.broadcasted_iota(jnp.int32, (sub, sub), 0)
    ci = lax.broadcasted_iota(jnp.int32, (sub, sub), 1)
    causal = ci < ri
    d0 = qi * tq
    heads_rows = [(h, a) for a in range(n_row) for h in range(2)]

    near_mask = jnp.broadcast_to(qi > 0, (sub, NEAR))
    tri_n = tri_ref[0:NEAR, 0:NEAR]
    tiles = [(h, a, keys_at(d0 + a * sub), vals_at(d0 + a * sub), causal, tri) for (h, a) in heads_rows]
    for (h, a) in heads_rows:
        s0 = pl.multiple_of(jnp.maximum(d0 + a * sub - NEAR, 0), NEAR)
        tiles.append((h, a, k_ref[0, pl.ds(s0, NEAR), :], v_ref[0, pl.ds(s0, NEAR), :],
                      near_mask if a == 0 else None, tri_n))
    carries, lowest = run(tiles, {ha: jnp.zeros((sub, 1), F32) for ha in heads_rows})

    n_walk = qi * n_row + n_row - 1

    def more(state):
        return jnp.logical_and(state[0] < n_walk, state[1] < CARRY_STOP)

    def body(state):
        i = state[0]
        kb = n_walk - 1 - i
        kblk, vblk = keys_at(kb * sub), vals_at(kb * sub)
        key_pos = ci + kb * sub
        tiles = [(h, a, kblk, vblk, key_pos < d0 + a * sub - NEAR, tri) for (h, a) in heads_rows]
        carries, lowest = run(tiles, dict(zip(heads_rows, state[2:])))
        return (i + 1, lowest) + tuple(carries[ha] for ha in heads_rows)

    @pl.when(lowest < CARRY_STOP)
    def _():
        state = (jnp.int32(0), lowest) + tuple(carries[ha] for ha in heads_rows)
        state = lax.while_loop(more, body, state)

        @pl.when(state[1] < CARRY_STOP)
        def _():
            meta_mask = lax.broadcasted_iota(jnp.int32, (sub, LANES), 1) < n_meta
            tri_m = tri_ref[0:LANES, 0:LANES]
            run([(h, a, km_ref[...], vm_ref[...], meta_mask, tri_m) for (h, a) in heads_rows],
                dict(zip(heads_rows, state[2:])))

    o_ref[0] = jnp.where(lane < HEAD_DIM, acc_ref[0], acc_ref[1]).astype(o_ref.dtype)


def _sb_attn(qkv, qkv_meta, n_meta, tq, sub):
    b, s, w3 = qkv.shape
    width = w3 // 3
    pairs = width // LANES
    assert s % tq == 0 and tq % sub == 0 and sub % LANES == 0
    tri = (jnp.arange(sub)[:, None] >= jnp.arange(sub)[None, :]).astype(BF16)
    return pl.pallas_call(
        functools.partial(_sb_kernel, tq=tq, sub=sub, n_meta=n_meta),
        out_shape=jax.ShapeDtypeStruct((b, s, width), BF16),
        grid=(b, pairs, s // tq),
        in_specs=[
            pl.BlockSpec((1, tq, LANES), lambda bi, p, qi: (bi, qi, p)),
            pl.BlockSpec((1, s, LANES), lambda bi, p, qi: (bi, 0, pairs + p)),
            pl.BlockSpec((1, s, LANES), lambda bi, p, qi: (bi, 0, 2 * pairs + p)),
            pl.BlockSpec((LANES, LANES), lambda bi, p, qi: (0, pairs + p)),
            pl.BlockSpec((LANES, LANES), lambda bi, p, qi: (0, 2 * pairs + p)),
            pl.BlockSpec((sub, sub), lambda bi, p, qi: (0, 0)),
        ],
        out_specs=pl.BlockSpec((1, tq, LANES), lambda bi, p, qi: (bi, qi, p)),
        scratch_shapes=[pltpu.VMEM((2, tq, LANES), F32)],
        compiler_params=pltpu.CompilerParams(
            dimension_semantics=("parallel", "parallel", "arbitrary"),
            vmem_limit_bytes=VMEM_LIMIT),
        name="sb_attn",
    )(qkv, qkv, qkv, qkv_meta, qkv_meta, tri)


P_LORA = 1
P_CUMSUM = 2
P_MEAN = 2
P_SUM = 1


def _rwkv_kernel(ux_ref, um_ref, mu_ref, wlo_ref, dbase_ref, abase_ref, gup_ref, kk_ref, ka_ref,
                 rk_ref, lng_ref, lnb_ref, bd_ref, tril_ref, o_ref, st_ref, carry_ref, y_ref,
                 *, width):
    c = pl.program_id(1)
    R, C = RW_ROWS, CHUNK
    n_chunks = R // C

    @pl.when(c == 0)
    def _():
        st_ref[...] = jnp.zeros_like(st_ref)
        carry_ref[...] = jnp.zeros_like(carry_ref)

    n_seq = ux_ref.shape[0]
    pairs = width // LANES
    row0 = lax.broadcasted_iota(jnp.int32, (SUBLANES, ux_ref.shape[2]), 0) == 0
    lane_wa = lax.broadcasted_iota(jnp.int32, (R, LANES), 1)
    shifted_cache = {}

    def shifted(bi):
        if bi not in shifted_cache:
            u = jnp.where(c == 0, um_ref[...], ux_ref[bi])
            rolled = pltpu.roll(u, 1, axis=0)
            prev = jnp.concatenate(
                [jnp.where(row0, carry_ref[bi], rolled[0:SUBLANES]), rolled[SUBLANES:]], axis=0)
            carry_ref[bi] = u[R - 1:R, :]
            us = u + mu_ref[...] * (prev - u)
            xwa = us[:, 3 * width:3 * width + LANES]
            xg = us[:, 3 * width + LANES:3 * width + 2 * LANES]
            twa = jnp.where(lane_wa < HEAD_DIM, jnp.tanh(xwa), xwa)
            shifted_cache[bi] = (us, twa, jax.nn.sigmoid(xg))
        return shifted_cache[bi]

    bd = bd_ref[...]
    tril = tril_ref[...]

    ri = lax.broadcasted_iota(jnp.int32, (R, R), 0)
    ci = lax.broadcasted_iota(jnp.int32, (R, R), 1)
    same = (ri // C) == (ci // C)
    strict = same & (ci < ri)
    incl = same & (ci <= ri)
    eye = ci == ri
    head0 = (lax.broadcasted_iota(jnp.int32, (R, LANES), 1) < HEAD_DIM)
    head0_2 = jnp.concatenate([head0, head0], axis=1)
    hmask = (head0, ~head0)
    si = lax.broadcasted_iota(jnp.int32, (LANES, LANES), 0)
    sj = lax.broadcasted_iota(jnp.int32, (LANES, LANES), 1)
    st_same = (si // HEAD_DIM) == (sj // HEAD_DIM)
    st_eye = si == sj
    zero_b = jnp.zeros((R, LANES), BF16)

    def packed(x_bd):
        out = x_bd[0:C]
        for j in range(1, n_chunks):
            out = out + x_bd[j * C:(j + 1) * C]
        return out

    def block_diag(x_p):
        return jnp.where(same, jnp.concatenate([x_p.astype(BF16)] * n_chunks, axis=0), jnp.zeros((R, R), BF16))

    eye_p = packed(jnp.where(eye, 1.0, 0.0))

    def stages(bi):
        gw = width
        cols = slice(0, width)
        at = lambda off: slice(off, off + width)
        heads = [(q, h) for q in range(pairs) for h in range(2)]
        sls = [slice(q * LANES, (q + 1) * LANES) for q in range(pairs)]
        us, twa, sig_g = shifted(bi)

        rr, rk, rv = us[:, at(0)], us[:, at(width)], us[:, at(2 * width)]
        dec_pre = dbase_ref[:, cols] + _mm(twa, wlo_ref[:, at(0)], pa=P_LORA, pb=P_LORA)
        ld = -DECAY_SCALE * jax.nn.sigmoid(dec_pre)
        a = jax.nn.sigmoid(abase_ref[:, cols] + _mm(twa, wlo_ref[:, at(width)], pa=P_LORA, pb=P_LORA))
        gate = _mm(sig_g, gup_ref[:, cols], pa=P_LORA, pb=P_LORA)
        kkr = rk * kk_ref[:, cols]
        ss = _mm(kkr * kkr, bd, pa=P_SUM)
        kk = kkr * lax.rsqrt(jnp.maximum(ss, KK_EPS * KK_EPS))
        k = rk * (1.0 + (a - 1.0) * ka_ref[:, cols])
        beta = kk * a
        yield

        cs = _mm(tril, ld, pb=P_CUMSUM)
        d_last = [jnp.exp(cs[(j + 1) * C - 1:(j + 1) * C, :]) for j in range(n_chunks)]
        e_in = jnp.exp(cs)
        e_neg = 1.0 / e_in
        e_last = jnp.concatenate([jnp.broadcast_to(d, (C, gw)) for d in d_last], axis=0) * e_neg
        a_t = (-kk * jnp.exp(cs - ld)).astype(BF16)
        r_t = rr * e_in
        r_b = r_t.astype(BF16)
        b_t = (beta * e_neg).astype(BF16)
        k_t = (k * e_neg).astype(BF16)
        b_h = (beta * e_last).astype(BF16)
        k_h = (k * e_last).astype(BF16)
        v_b = rv.astype(BF16)
        yield

        l_ab, l_ak, p_rb, p_rk = {}, {}, {}, {}
        for (q, h) in heads:
            Ah = jnp.where(hmask[h], a_t[:, sls[q]], zero_b)
            Rh = jnp.where(hmask[h], r_b[:, sls[q]], zero_b)
            l_ab[q, h] = jnp.where(strict, _bdot(Ah, b_t[:, sls[q]], NT), 0.0)
            l_ak[q, h] = jnp.where(strict, _bdot(Ah, k_t[:, sls[q]], NT), 0.0).astype(BF16)
            p_rb[q, h] = jnp.where(incl, _bdot(Rh, b_t[:, sls[q]], NT), 0.0).astype(BF16)
            p_rk[q, h] = jnp.where(incl, _bdot(Rh, k_t[:, sls[q]], NT), 0.0).astype(BF16)
        yield

        pw_bd = {qh: l_ab[qh].astype(BF16) for qh in heads}
        pw = {qh: packed(l_ab[qh]) for qh in heads}
        ti = {qh: eye_p + pw[qh] for qh in heads}
        for qh in heads:
            pw[qh] = _bdot(pw[qh], pw_bd[qh])
        yield
        n_levels = 5
        for lvl in range(n_levels):
            last = lvl == n_levels - 1
            for qh in heads:
                pw_bd[qh] = block_diag(pw[qh])
                if last:
                    ti[qh] = ti[qh] + _bdot(ti[qh], pw_bd[qh])
                else:
                    both = _bdot(jnp.concatenate([ti[qh], pw[qh]], axis=0), pw_bd[qh])
                    ti[qh] = ti[qh] + both[0:C]
                    pw[qh] = both[C:2 * C]
            yield

        t_inv = {qh: block_diag(ti[qh]) for qh in heads}
        lv = {qh: _bdot(l_ak[qh], v_b[:, sls[qh[0]]]) for qh in heads}
        wu, ya, yb = {}, {}, {}
        for q in range(pairs):
            lv2 = jnp.where(head0, lv[q, 0], lv[q, 1]).astype(BF16)
            rhs = jnp.concatenate([a_t[:, sls[q]], lv2], axis=1)
            wu[q] = jnp.where(head0_2, _bdot(t_inv[q, 0], rhs), _bdot(t_inv[q, 1], rhs)).astype(BF16)
        yield
        for q in range(pairs):
            pwu = jnp.where(head0_2, _bdot(p_rb[q, 0], wu[q]), _bdot(p_rb[q, 1], wu[q]))
            pv = jnp.where(head0, _bdot(p_rk[q, 0], v_b[:, sls[q]]), _bdot(p_rk[q, 1], v_b[:, sls[q]]))
            ya[q] = (r_t[:, sls[q]] + pwu[:, :LANES]).astype(BF16)
            yb[q] = pwu[:, LANES:] + pv
        yield

        gm, hm_ = {}, {}
        for j in range(n_chunks):
            rs = slice(j * C, (j + 1) * C)
            for q in range(pairs):
                gh = _bdot(b_h[rs, sls[q]], wu[q][rs], TN)
                kv = _bdot(k_h[rs, sls[q]], v_b[rs, sls[q]], TN)
                dl = jnp.broadcast_to(d_last[j][:, sls[q]], (LANES, LANES))
                gm[q, j] = (jnp.where(st_same, gh[:, :LANES], 0.0) + jnp.where(st_eye, dl, 0.0)).astype(BF16)
                hm_[q, j] = jnp.where(st_same, gh[:, LANES:] + kv, 0.0)
        yield

        slot = lambda q: bi * pairs + q
        m = {q: st_ref[slot(q)] for q in range(pairs)}
        for j in range(n_chunks):
            rs = slice(j * C, (j + 1) * C)
            for q in range(pairs):
                mb = m[q].astype(BF16)
                y_ref[bi, rs, sls[q]] = _bdot(ya[q][rs], mb) + yb[q][rs]
                m[q] = _bdot(gm[q, j], mb) + hm_[q, j]
        for q in range(pairs):
            st_ref[slot(q)] = m[q]
        yield

        y = y_ref[bi, :, cols]
        inv_n = 1.0 / HEAD_DIM
        mean = _mm(y, bd, pa=P_MEAN) * inv_n
        yc = y - mean
        var = _mm(yc * yc, bd, pa=P_SUM) * inv_n
        yn = yc * lax.rsqrt(var + GN_EPS) * lng_ref[:, cols] + lnb_ref[:, cols]
        bonus = _mm(rr * k * rk_ref[:, cols], bd, pa=1) * rv
        o_ref[bi, :, cols] = ((yn + bonus) * gate).astype(o_ref.dtype)

    gens = [stages(bi) for bi in range(n_seq)]
    live = True
    while live:
        live = all([next(gen, "done") != "done" for gen in gens])


def _rwkv(u_x, u_meta, mu, w_lora, dbase, abase, gate_up, k_k, k_a, r_k, ln_g, ln_b):
    b, s, uw = u_x.shape
    width = dbase.shape[1]
    assert s % RW_ROWS == 0 and uw == 3 * width + 2 * LANES and width % LANES == 0
    nc = s // RW_ROWS + 1
    gw = width
    hid = jnp.arange(gw) // HEAD_DIM
    bd = (hid[:, None] == hid[None, :]).astype(BF16)
    t = jnp.arange(RW_ROWS)
    tril = ((t[:, None] // CHUNK == t[None, :] // CHUNK) & (t[:, None] >= t[None, :])).astype(BF16)
    const = lambda shape: pl.BlockSpec(shape, lambda bi, c: tuple(0 for _ in shape))
    n_seq = RW_SEQS if b % RW_SEQS == 0 else 1
    return pl.pallas_call(
        functools.partial(_rwkv_kernel, width=width),
        out_shape=jax.ShapeDtypeStruct((b, s, width), BF16),
        grid=(b // n_seq, nc),
        in_specs=[
            pl.BlockSpec((n_seq, RW_ROWS, uw), lambda bi, c: (bi, jnp.maximum(c - 1, 0), 0)),
            const((RW_ROWS, uw)), const((1, uw)), const((LANES, 2 * width)), const((1, width)),
            const((1, width)), const((LANES, width)), const((1, width)), const((1, width)),
            const((1, width)), const((1, width)), const((1, width)), const((gw, gw)),
            const((RW_ROWS, RW_ROWS)),
        ],
        out_specs=pl.BlockSpec((n_seq, RW_ROWS, width), lambda bi, c: (bi, jnp.maximum(c - 1, 0), 0)),
        scratch_shapes=[
            pltpu.VMEM((n_seq * (width // LANES), LANES, LANES), F32),
            pltpu.VMEM((n_seq, 1, uw), F32),
            pltpu.VMEM((n_seq, RW_ROWS, width), F32),
        ],
        compiler_params=pltpu.CompilerParams(
            dimension_semantics=("parallel", "arbitrary"), vmem_limit_bytes=VMEM_LIMIT),
        name="rwkv",
    )(u_x, u_meta, mu, w_lora, dbase, abase, gate_up, k_k, k_a, r_k, ln_g, ln_b, bd, tril)


def _merge_ffn_kernel(x_ref, osb_ref, orw_ref, gsb_ref, grw_ref, wsb_ref, wrw_ref, wout_ref, gmix_ref,
                      gpre_ref, wg_ref, wu_ref, wd_ref, gpost_ref, o_ref, *, ff_chunks):
    a = jnp.dot(osb_ref[...], wsb_ref[...], preferred_element_type=F32)
    b = jnp.dot(orw_ref[...], wrw_ref[...], preferred_element_type=F32)
    merged = gsb_ref[...].astype(F32) * a + grw_ref[...].astype(F32) * b
    y = jnp.dot(merged.astype(BF16), wout_ref[...], preferred_element_type=F32)
    h1 = x_ref[...] + _rms(y, gmix_ref[...])
    xn = _rms(h1, gpre_ref[...]).astype(BF16)
    f = None
    for c0, c1 in ff_chunks:
        gate = jnp.dot(xn, wg_ref[:, c0:c1], preferred_element_type=F32)
        up = jnp.dot(xn, wu_ref[:, c0:c1], preferred_element_type=F32)
        act = (gate * jax.nn.sigmoid(gate) * up).astype(BF16)
        t = jnp.dot(act, wd_ref[c0:c1, :], preferred_element_type=F32)
        f = t if f is None else f + t
    o_ref[...] = h1 + _rms(f, gpost_ref[...])


def _merge_ffn(x2d, o_sb, o_rw, gates, w_sb, w_rw, w_out, g_mix, g_pre, w_gate, w_up, w_down, g_post, tm):
    m, d = x2d.shape
    width = o_sb.shape[1]
    ff = w_gate.shape[1]
    assert m % tm == 0 and ff % LANES == 0
    edges = list(range(0, ff, FF_CHUNK)) + [ff]
    ff_chunks = tuple(zip(edges[:-1], edges[1:]))
    row = lambda cols, jb=0: pl.BlockSpec((tm, cols), lambda i: (i, jb))
    return pl.pallas_call(
        functools.partial(_merge_ffn_kernel, ff_chunks=ff_chunks),
        out_shape=jax.ShapeDtypeStruct((m, d), F32),
        grid=(m // tm,),
        in_specs=[row(d), row(width), row(width), row(d, 0), row(d, 1),
                  _resident((width, d)), _resident((width, d)), _resident((d, d)), _resident((1, d)),
                  _resident((1, d)), _resident((d, ff)), _resident((d, ff)), _resident((ff, d)),
                  _resident((1, d))],
        out_specs=row(d),
        compiler_params=pltpu.CompilerParams(
            dimension_semantics=("parallel",), vmem_limit_bytes=VMEM_LIMIT),
        name="merge_ffn",
    )(x2d, o_sb, o_rw, gates, gates, w_sb, w_rw, w_out, g_mix, g_pre, w_gate, w_up, w_down, g_post)


def _pick(n, prefs):
    for t in prefs:
        if n % t == 0:
            return t
    return n


def kernel(x, meta_tokens, norm_mix_pre, norm_mix_post, w_in, rw_shift_mu, rw_decay_up, rw_decay_base,
           rw_aaa_up, rw_aaa_base, rw_gate_up, rw_k_k, rw_k_a, rw_r_k, rw_ln_gain, rw_ln_bias,
           w_branch_sb, w_branch_rw, w_out, norm_ffn_pre, norm_ffn_post, w_ffn_gate, w_ffn_up,
           w_ffn_down):
    b, s, d = x.shape
    n_meta = meta_tokens.shape[0]
    depth = w_in.shape[0]
    assert depth == 1, "meta rows are only carried through the mixer of a single layer"
    width = w_branch_sb.shape[1]
    dlora = rw_decay_up.shape[1]
    alora = rw_aaa_up.shape[1]
    glora = rw_gate_up.shape[1]
    assert dlora + alora == LANES and glora == LANES and n_meta <= RW_ROWS
    l = 0
    m = b * s
    x2d = x.reshape(m, d)

    c_sb, c_rw = 3 * width, 3 * width + 3 * width + 2 * LANES
    w_in_b = w_in[l].astype(BF16)
    g_pre = norm_mix_pre[l][None, :]

    qkv, u_rw, gates = _proj(x2d, g_pre, w_in_b, c_sb, c_rw, _pick(m, (1024, 512, 256, 128)), "proj")
    meta_pad = jnp.zeros((LANES, d), F32).at[:n_meta].set(meta_tokens.astype(F32))
    qkv_meta, rw_meta, _ = _proj(meta_pad, g_pre, w_in_b, c_sb, c_rw, LANES, "proj_meta")
    row_ok = (jnp.arange(LANES) < n_meta)[:, None]
    qkv_meta = jnp.where(row_ok, qkv_meta, jnp.zeros_like(qkv_meta))
    u_meta = jnp.zeros((RW_ROWS, c_rw - c_sb), F32).at[RW_ROWS - n_meta:].set(rw_meta[:n_meta])

    tq = _pick(s, (1024, 512, 256, 128))
    o_sb = _sb_attn(qkv.reshape(b, s, c_sb), qkv_meta, n_meta, tq, min(tq, SB_TILE))

    w_lora = jnp.zeros((LANES, 2 * width), F32)
    w_lora = w_lora.at[:dlora, :width].set(rw_decay_up[l]).at[dlora:, width:].set(rw_aaa_up[l])
    vec = lambda p: p[l].reshape(1, -1).astype(F32)
    o_rw = _rwkv(u_rw.reshape(b, s, c_rw - c_sb), u_meta, vec(rw_shift_mu), w_lora,
                 vec(rw_decay_base), vec(rw_aaa_base), rw_gate_up[l].astype(F32), vec(rw_k_k),
                 vec(rw_k_a), vec(rw_r_k), vec(rw_ln_gain), vec(rw_ln_bias))

    out = _merge_ffn(x2d, o_sb.reshape(m, width), o_rw.reshape(m, width), gates,
                     w_branch_sb[l].astype(BF16), w_branch_rw[l].astype(BF16), w_out[l].astype(BF16),
                     norm_mix_post[l][None, :], norm_ffn_pre[l][None, :], w_ffn_gate[l].astype(BF16),
                     w_ffn_up[l].astype(BF16), w_ffn_down[l].astype(BF16), norm_ffn_post[l][None, :],
                     _pick(m, (512, 256, 128)))
    return out.reshape(b, s, d)
```

```python
import functools

import jax
import jax.numpy as jnp
from jax import lax
from jax.experimental import pallas as pl
from jax.experimental.pallas import tpu as pltpu

HEAD_DIM = 64
RMS_EPS = 1e-6
GN_EPS = 64e-5
KK_EPS = 1e-12
LANES = 128
SUBLANES = 8
DECAY_SCALE = 0.6065306597126334
CHUNK = 64
RW_ROWS = 256
RW_SEQS = 2
FF_CHUNK = 1024
VMEM_LIMIT = 56 * 1024 * 1024
LOG2E = 1.4426950408889634
MASKED = 1e30
CARRY_STOP = 64.0
SB_TILE = 256
NEAR = 128

F32 = jnp.float32
BF16 = jnp.bfloat16

NN = (((1,), (0,)), ((), ()))
NT = (((1,), (1,)), ((), ()))
TN = (((0,), (0,)), ((), ()))


def _split(x, n):
    if x.dtype == BF16:
        return [x]
    parts = []
    r = x
    for i in range(n):
        p = r.astype(BF16)
        parts.append(p)
        if i + 1 < n:
            r = r - p.astype(F32)
    return parts


def _mm(a, b, dims=NN, pa=1, pb=1):
    a_parts = _split(a, pa)
    b_parts = _split(b, pb)
    order = max(len(a_parts), len(b_parts))
    acc = None
    for i, ai in enumerate(a_parts):
        for j, bj in enumerate(b_parts):
            if i + j >= order:
                continue
            t = lax.dot_general(ai, bj, dims, preferred_element_type=F32)
            acc = t if acc is None else acc + t
    return acc


def _bdot(a, b, dims=NN):
    return lax.dot_general(a.astype(BF16), b.astype(BF16), dims, preferred_element_type=F32)


def _rms(x, g):
    ms = jnp.mean(x * x, axis=-1, keepdims=True)
    return x * lax.rsqrt(ms + RMS_EPS) * g


def _resident(shape):
    return pl.BlockSpec(shape, lambda i: (0,) * len(shape), pipeline_mode=pl.Buffered(1))


def _proj_kernel(x_ref, g_ref, w_ref, qkv_ref, rw_ref, gate_ref, *, c_sb, c_rw):
    xn = _rms(x_ref[...], g_ref[...]).astype(BF16)
    qkv_ref[...] = jnp.dot(xn, w_ref[:, 0:c_sb], preferred_element_type=F32).astype(qkv_ref.dtype)
    rw_ref[...] = jnp.dot(xn, w_ref[:, c_sb:c_rw], preferred_element_type=F32)
    gate_ref[...] = jax.nn.sigmoid(
        jnp.dot(xn, w_ref[:, c_rw:], preferred_element_type=F32)).astype(gate_ref.dtype)


def _proj(x2d, gain, w, c_sb, c_rw, tm, name):
    m, d = x2d.shape
    n = w.shape[1]
    assert m % tm == 0
    row = lambda cols: pl.BlockSpec((tm, cols), lambda i: (i, 0))
    return pl.pallas_call(
        functools.partial(_proj_kernel, c_sb=c_sb, c_rw=c_rw),
        out_shape=(jax.ShapeDtypeStruct((m, c_sb), BF16),
                   jax.ShapeDtypeStruct((m, c_rw - c_sb), F32),
                   jax.ShapeDtypeStruct((m, n - c_rw), BF16)),
        grid=(m // tm,),
        in_specs=[row(d), _resident((1, d)), _resident((d, n))],
        out_specs=(row(c_sb), row(c_rw - c_sb), row(n - c_rw)),
        compiler_params=pltpu.CompilerParams(
            dimension_semantics=("parallel",), vmem_limit_bytes=VMEM_LIMIT),
        name=name,
    )(x2d, gain, w)


def _sb_kernel(q_ref, k_ref, v_ref, km_ref, vm_ref, tri_ref, o_ref, acc_ref, *, tq, sub, n_meta):
    qi = pl.program_id(2)
    lane = lax.broadcasted_iota(jnp.int32, (tq, LANES), 1)
    q2 = (q_ref[0].astype(F32) * (HEAD_DIM ** -0.5 * LOG2E)).astype(BF16)
    zero = jnp.zeros_like(q2)
    qh = (jnp.where(lane < HEAD_DIM, q2, zero), jnp.where(lane >= HEAD_DIM, q2, zero))
    tri = tri_ref[...]

    n_row = tq // sub
    rows = lambda a: slice(a * sub, (a + 1) * sub)

    def start(h, a, kblk, mask, tri_b):
        z = lax.dot_general(qh[h][rows(a)], kblk, NT, preferred_element_type=F32)
        if mask is not None:
            z = jnp.where(mask, z, -MASKED)
        sp = jnp.maximum(z, 0.0) + jnp.log2(1.0 + jnp.exp2(-jnp.abs(z)))
        return z, jnp.dot(sp.astype(BF16), tri_b, preferred_element_type=F32)

    def finish(h, a, zc, carry, vblk):
        z, cs = zc
        c = cs + carry
        acc_ref[h, rows(a), :] += jnp.dot(jnp.exp2(z - c).astype(BF16), vblk, preferred_element_type=F32)

    def run(tiles, carries):
        carries = dict(carries)
        pending = []
        for (h, a, kblk, vblk, mask, tri_b) in tiles:
            zc = start(h, a, kblk, mask, tri_b)
            pending.append((h, a, zc, carries[h, a], vblk))
            carries[h, a] = zc[1][:, 0:1] + carries[h, a]
        low_all, low_walk = None, None
        for (h, a), c in carries.items():
            low_all = c if low_all is None else jnp.minimum(low_all, c)
            cw = jnp.where(qi > 0, c, MASKED) if a == 0 else c
            low_walk = cw if low_walk is None else jnp.minimum(low_walk, cw)
        low_all, low_walk = jnp.min(low_all), jnp.min(low_walk)
        for item in pending:
            finish(*item)
        return carries, low_walk, low_all

    acc_ref[...] = jnp.zeros_like(acc_ref)
    keys_at = lambda s0: k_ref[0, pl.ds(pl.multiple_of(s0, sub), sub), :]
    vals_at = lambda s0: v_ref[0, pl.ds(pl.multiple_of(s0, sub), sub), :]
    ri = lax.broadcasted_iota(jnp.int32, (sub, sub), 0)
    ci = lax.broadcasted_iota(jnp.int32, (sub, sub), 1)
    causal = ci < ri
    d0 = qi * tq
    heads_rows = [(h, a) for a in range(n_row) for h in range(2)]

    near_mask = jnp.broadcast_to(qi > 0, (sub, NEAR))
    tri_n = tri_ref[0:NEAR, 0:NEAR]
    tiles = [(h, a, keys_at(d0 + a * sub), vals_at(d0 + a * sub), causal, tri) for (h, a) in heads_rows]
    for (h, a) in heads_rows:
        s0 = pl.multiple_of(jnp.maximum(d0 + a * sub - NEAR, 0), NEAR)
        tiles.append((h, a, k_ref[0, pl.ds(s0, NEAR), :], v_ref[0, pl.ds(s0, NEAR), :],
                      near_mask if a == 0 else None, tri_n))
    carries, low_walk, low_all = run(tiles, {ha: jnp.zeros((sub, 1), F32) for ha in heads_rows})

    n_walk = qi * n_row + n_row - 1

    def more(state):
        return jnp.logical_and(state[0] < n_walk, state[1] < CARRY_STOP)

    def body(state):
        i = state[0]
        kb = n_walk - 1 - i
        kblk, vblk = keys_at(kb * sub), vals_at(kb * sub)
        key_pos = ci + kb * sub
        tiles = [(h, a, kblk, vblk, key_pos < d0 + a * sub - NEAR, tri) for (h, a) in heads_rows]
        carries, low_walk, low_all = run(tiles, dict(zip(heads_rows, state[3:])))
        return (i + 1, low_walk, low_all) + tuple(carries[ha] for ha in heads_rows)

    @pl.when(low_all < CARRY_STOP)
    def _():
        state = (jnp.int32(0), low_walk, low_all) + tuple(carries[ha] for ha in heads_rows)
        state = lax.while_loop(more, body, state)

        @pl.when(state[2] < CARRY_STOP)
        def _():
            meta_mask = lax.broadcasted_iota(jnp.int32, (sub, LANES), 1) < n_meta
            tri_m = tri_ref[0:LANES, 0:LANES]
            run([(h, a, km_ref[...], vm_ref[...], meta_mask, tri_m) for (h, a) in heads_rows],
                dict(zip(heads_rows, state[3:])))

    o_ref[0] = jnp.where(lane < HEAD_DIM, acc_ref[0], acc_ref[1]).astype(o_ref.dtype)


def _sb_attn(qkv, qkv_meta, n_meta, tq, sub):
    b, s, w3 = qkv.shape
    width = w3 // 3
    pairs = width // LANES
    assert s % tq == 0 and tq % sub == 0 and sub % LANES == 0
    tri = (jnp.arange(sub)[:, None] >= jnp.arange(sub)[None, :]).astype(BF16)
    return pl.pallas_call(
        functools.partial(_sb_kernel, tq=tq, sub=sub, n_meta=n_meta),
        out_shape=jax.ShapeDtypeStruct((b, s, width), BF16),
        grid=(b, pairs, s // tq),
        in_specs=[
            pl.BlockSpec((1, tq, LANES), lambda bi, p, qi: (bi, qi, p)),
            pl.BlockSpec((1, s, LANES), lambda bi, p, qi: (bi, 0, pairs + p)),
            pl.BlockSpec((1, s, LANES), lambda bi, p, qi: (bi, 0, 2 * pairs + p)),
            pl.BlockSpec((LANES, LANES), lambda bi, p, qi: (0, pairs + p)),
            pl.BlockSpec((LANES, LANES), lambda bi, p, qi: (0, 2 * pairs + p)),
            pl.BlockSpec((sub, sub), lambda bi, p, qi: (0, 0)),
        ],
        out_specs=pl.BlockSpec((1, tq, LANES), lambda bi, p, qi: (bi, qi, p)),
        scratch_shapes=[pltpu.VMEM((2, tq, LANES), F32)],
        compiler_params=pltpu.CompilerParams(
            dimension_semantics=("parallel", "parallel", "arbitrary"),
            vmem_limit_bytes=VMEM_LIMIT),
        name="sb_attn",
    )(qkv, qkv, qkv, qkv_meta, qkv_meta, tri)


P_LORA = 1
P_CUMSUM = 2
P_MEAN = 2
P_SUM = 1


def _rwkv_kernel(ux_ref, um_ref, mu_ref, wlo_ref, dbase_ref, abase_ref, gup_ref, kk_ref, ka_ref,
                 rk_ref, lng_ref, lnb_ref, bd_ref, tril_ref, o_ref, st_ref, carry_ref, y_ref,
                 *, width):
    c = pl.program_id(1)
    R, C = RW_ROWS, CHUNK
    n_chunks = R // C

    @pl.when(c == 0)
    def _():
        st_ref[...] = jnp.zeros_like(st_ref)
        carry_ref[...] = jnp.zeros_like(carry_ref)

    n_seq = ux_ref.shape[0]
    pairs = width // LANES
    row0 = lax.broadcasted_iota(jnp.int32, (SUBLANES, ux_ref.shape[2]), 0) == 0
    lane_wa = lax.broadcasted_iota(jnp.int32, (R, LANES), 1)
    shifted_cache = {}

    def shifted(bi):
        if bi not in shifted_cache:
            u = jnp.where(c == 0, um_ref[...], ux_ref[bi])
            rolled = pltpu.roll(u, 1, axis=0)
            prev = jnp.concatenate(
                [jnp.where(row0, carry_ref[bi], rolled[0:SUBLANES]), rolled[SUBLANES:]], axis=0)
            carry_ref[bi] = u[R - 1:R, :]
            us = u + mu_ref[...] * (prev - u)
            xwa = us[:, 3 * width:3 * width + LANES]
            xg = us[:, 3 * width + LANES:3 * width + 2 * LANES]
            twa = jnp.where(lane_wa < HEAD_DIM, jnp.tanh(xwa), xwa)
            shifted_cache[bi] = (us, twa, jax.nn.sigmoid(xg))
        return shifted_cache[bi]

    bd = bd_ref[...]
    tril = tril_ref[...]

    ri = lax.broadcasted_iota(jnp.int32, (R, R), 0)
    ci = lax.broadcasted_iota(jnp.int32, (R, R), 1)
    same = (ri // C) == (ci // C)
    strict = same & (ci < ri)
    incl = same & (ci <= ri)
    eye = ci == ri
    head0 = (lax.broadcasted_iota(jnp.int32, (R, LANES), 1) < HEAD_DIM)
    head0_2 = jnp.concatenate([head0, head0], axis=1)
    hmask = (head0, ~head0)
    si = lax.broadcasted_iota(jnp.int32, (LANES, LANES), 0)
    sj = lax.broadcasted_iota(jnp.int32, (LANES, LANES), 1)
    st_same = (si // HEAD_DIM) == (sj // HEAD_DIM)
    st_eye = si == sj
    zero_b = jnp.zeros((R, LANES), BF16)

    def packed(x_bd):
        out = x_bd[0:C]
        for j in range(1, n_chunks):
            out = out + x_bd[j * C:(j + 1) * C]
        return out

    def block_diag(x_p):
        return jnp.where(same, jnp.concatenate([x_p.astype(BF16)] * n_chunks, axis=0), jnp.zeros((R, R), BF16))

    eye_p = packed(jnp.where(eye, 1.0, 0.0))

    def stages(bi):
        gw = width
        cols = slice(0, width)
        at = lambda off: slice(off, off + width)
        heads = [(q, h) for q in range(pairs) for h in range(2)]
        sls = [slice(q * LANES, (q + 1) * LANES) for q in range(pairs)]
        us, twa, sig_g = shifted(bi)

        rr, rk, rv = us[:, at(0)], us[:, at(width)], us[:, at(2 * width)]
        dec_pre = dbase_ref[:, cols] + _mm(twa, wlo_ref[:, at(0)], pa=P_LORA, pb=P_LORA)
        ld = -DECAY_SCALE * jax.nn.sigmoid(dec_pre)
        a = jax.nn.sigmoid(abase_ref[:, cols] + _mm(twa, wlo_ref[:, at(width)], pa=P_LORA, pb=P_LORA))
        gate = _mm(sig_g, gup_ref[:, cols], pa=P_LORA, pb=P_LORA)
        kkr = rk * kk_ref[:, cols]
        ss = _mm(kkr * kkr, bd, pa=P_SUM)
        kk = kkr * lax.rsqrt(jnp.maximum(ss, KK_EPS * KK_EPS))
        k = rk * (1.0 + (a - 1.0) * ka_ref[:, cols])
        beta = kk * a
        yield

        cs = _mm(tril, ld, pb=P_CUMSUM)
        d_last = [jnp.exp(cs[(j + 1) * C - 1:(j + 1) * C, :]) for j in range(n_chunks)]
        e_in = jnp.exp(cs)
        e_neg = 1.0 / e_in
        e_last = jnp.concatenate([jnp.broadcast_to(d, (C, gw)) for d in d_last], axis=0) * e_neg
        a_t = (-kk * jnp.exp(cs - ld)).astype(BF16)
        r_t = rr * e_in
        r_b = r_t.astype(BF16)
        b_t = (beta * e_neg).astype(BF16)
        k_t = (k * e_neg).astype(BF16)
        b_h = (beta * e_last).astype(BF16)
        k_h = (k * e_last).astype(BF16)
        v_b = rv.astype(BF16)
        yield

        l_ab, l_ak, p_rb, p_rk = {}, {}, {}, {}
        for (q, h) in heads:
            Ah = jnp.where(hmask[h], a_t[:, sls[q]], zero_b)
            Rh = jnp.where(hmask[h], r_b[:, sls[q]], zero_b)
            l_ab[q, h] = jnp.where(strict, _bdot(Ah, b_t[:, sls[q]], NT), 0.0)
            l_ak[q, h] = jnp.where(strict, _bdot(Ah, k_t[:, sls[q]], NT), 0.0).astype(BF16)
            p_rb[q, h] = jnp.where(incl, _bdot(Rh, b_t[:, sls[q]], NT), 0.0).astype(BF16)
            p_rk[q, h] = jnp.where(incl, _bdot(Rh, k_t[:, sls[q]], NT), 0.0).astype(BF16)
        yield

        pw_bd = {qh: l_ab[qh].astype(BF16) for qh in heads}
        pw = {qh: packed(l_ab[qh]) for qh in heads}
        ti = {qh: eye_p + pw[qh] for qh in heads}
        for qh in heads:
            pw[qh] = _bdot(pw[qh], pw_bd[qh])
        yield
        n_levels = 5
        for lvl in range(n_levels):
            last = lvl == n_levels - 1
            for qh in heads:
                pw_bd[qh] = block_diag(pw[qh])
                if last:
                    ti[qh] = ti[qh] + _bdot(ti[qh], pw_bd[qh])
                else:
                    both = _bdot(jnp.concatenate([ti[qh], pw[qh]], axis=0), pw_bd[qh])
                    ti[qh] = ti[qh] + both[0:C]
                    pw[qh] = both[C:2 * C]
            yield

        t_inv = {qh: block_diag(ti[qh]) for qh in heads}
        lv = {qh: _bdot(l_ak[qh], v_b[:, sls[qh[0]]]) for qh in heads}
        wu, ya, yb = {}, {}, {}
        for q in range(pairs):
            lv2 = jnp.where(head0, lv[q, 0], lv[q, 1]).astype(BF16)
            rhs = jnp.concatenate([a_t[:, sls[q]], lv2], axis=1)
            wu[q] = jnp.where(head0_2, _bdot(t_inv[q, 0], rhs), _bdot(t_inv[q, 1], rhs)).astype(BF16)
        yield
        for q in range(pairs):
            pwu = jnp.where(head0_2, _bdot(p_rb[q, 0], wu[q]), _bdot(p_rb[q, 1], wu[q]))
            pv = jnp.where(head0, _bdot(p_rk[q, 0], v_b[:, sls[q]]), _bdot(p_rk[q, 1], v_b[:, sls[q]]))
            ya[q] = (r_t[:, sls[q]] + pwu[:, :LANES]).astype(BF16)
            yb[q] = pwu[:, LANES:] + pv
        yield

        gm, hm_ = {}, {}
        for j in range(n_chunks):
            rs = slice(j * C, (j + 1) * C)
            for q in range(pairs):
                gh = _bdot(b_h[rs, sls[q]], wu[q][rs], TN)
                kv = _bdot(k_h[rs, sls[q]], v_b[rs, sls[q]], TN)
                dl = jnp.broadcast_to(d_last[j][:, sls[q]], (LANES, LANES))
                gm[q, j] = (jnp.where(st_same, gh[:, :LANES], 0.0) + jnp.where(st_eye, dl, 0.0)).astype(BF16)
                hm_[q, j] = jnp.where(st_same, gh[:, LANES:] + kv, 0.0)
        yield

        slot = lambda q: bi * pairs + q
        m = {q: st_ref[slot(q)] for q in range(pairs)}
        for j in range(n_chunks):
            rs = slice(j * C, (j + 1) * C)
            for q in range(pairs):
                mb = m[q].astype(BF16)
                y_ref[bi, rs, sls[q]] = _bdot(ya[q][rs], mb) + yb[q][rs]
                m[q] = _bdot(gm[q, j], mb) + hm_[q, j]
        for q in range(pairs):
            st_ref[slot(q)] = m[q]
        yield

        y = y_ref[bi, :, cols]
        inv_n = 1.0 / HEAD_DIM
        mean = _mm(y, bd, pa=P_MEAN) * inv_n
        yc = y - mean
        var = _mm(yc * yc, bd, pa=P_SUM) * inv_n
        yn = yc * lax.rsqrt(var + GN_EPS) * lng_ref[:, cols] + lnb_ref[:, cols]
        bonus = _mm(rr * k * rk_ref[:, cols], bd, pa=1) * rv
        o_ref[bi, :, cols] = ((yn + bonus) * gate).astype(o_ref.dtype)

    gens = [stages(bi) for bi in range(n_seq)]
    live = True
    while live:
        live = all([next(gen, "done") != "done" for gen in gens])


def _rwkv(u_x, u_meta, mu, w_lora, dbase, abase, gate_up, k_k, k_a, r_k, ln_g, ln_b):
    b, s, uw = u_x.shape
    width = dbase.shape[1]
    assert s % RW_ROWS == 0 and uw == 3 * width + 2 * LANES and width % LANES == 0
    nc = s // RW_ROWS + 1
    gw = width
    hid = jnp.arange(gw) // HEAD_DIM
    bd = (hid[:, None] == hid[None, :]).astype(BF16)
    t = jnp.arange(RW_ROWS)
    tril = ((t[:, None] // CHUNK == t[None, :] // CHUNK) & (t[:, None] >= t[None, :])).astype(BF16)
    const = lambda shape: pl.BlockSpec(shape, lambda bi, c: tuple(0 for _ in shape))
    n_seq = RW_SEQS if b % RW_SEQS == 0 else 1
    return pl.pallas_call(
        functools.partial(_rwkv_kernel, width=width),
        out_shape=jax.ShapeDtypeStruct((b, s, width), BF16),
        grid=(b // n_seq, nc),
        in_specs=[
            pl.BlockSpec((n_seq, RW_ROWS, uw), lambda bi, c: (bi, jnp.maximum(c - 1, 0), 0)),
            const((RW_ROWS, uw)), const((1, uw)), const((LANES, 2 * width)), const((1, width)),
            const((1, width)), const((LANES, width)), const((1, width)), const((1, width)),
            const((1, width)), const((1, width)), const((1, width)), const((gw, gw)),
            const((RW_ROWS, RW_ROWS)),
        ],
        out_specs=pl.BlockSpec((n_seq, RW_ROWS, width), lambda bi, c: (bi, jnp.maximum(c - 1, 0), 0)),
        scratch_shapes=[
            pltpu.VMEM((n_seq * (width // LANES), LANES, LANES), F32),
            pltpu.VMEM((n_seq, 1, uw), F32),
            pltpu.VMEM((n_seq, RW_ROWS, width), F32),
        ],
        compiler_params=pltpu.CompilerParams(
            dimension_semantics=("parallel", "arbitrary"), vmem_limit_bytes=VMEM_LIMIT),
        name="rwkv",
    )(u_x, u_meta, mu, w_lora, dbase, abase, gate_up, k_k, k_a, r_k, ln_g, ln_b, bd, tril)


def _merge_ffn_kernel(x_ref, osb_ref, orw_ref, gsb_ref, grw_ref, wsb_ref, wrw_ref, wout_ref, gmix_ref,
                      gpre_ref, wg_ref, wu_ref, wd_ref, gpost_ref, o_ref, *, ff_chunks):
    a = jnp.dot(osb_ref[...], wsb_ref[...], preferred_element_type=F32)
    b = jnp.dot(orw_ref[...], wrw_ref[...], preferred_element_type=F32)
    merged = gsb_ref[...].astype(F32) * a + grw_ref[...].astype(F32) * b
    y = jnp.dot(merged.astype(BF16), wout_ref[...], preferred_element_type=F32)
    h1 = x_ref[...] + _rms(y, gmix_ref[...])
    xn = _rms(h1, gpre_ref[...]).astype(BF16)
    f = None
    for c0, c1 in ff_chunks:
        gate = jnp.dot(xn, wg_ref[:, c0:c1], preferred_element_type=F32)
        up = jnp.dot(xn, wu_ref[:, c0:c1], preferred_element_type=F32)
        act = (gate * jax.nn.sigmoid(gate) * up).astype(BF16)
        t = jnp.dot(act, wd_ref[c0:c1, :], preferred_element_type=F32)
        f = t if f is None else f + t
    o_ref[...] = h1 + _rms(f, gpost_ref[...])


def _merge_ffn(x2d, o_sb, o_rw, gates, w_sb, w_rw, w_out, g_mix, g_pre, w_gate, w_up, w_down, g_post, tm):
    m, d = x2d.shape
    width = o_sb.shape[1]
    ff = w_gate.shape[1]
    assert m % tm == 0 and ff % LANES == 0
    edges = list(range(0, ff, FF_CHUNK)) + [ff]
    ff_chunks = tuple(zip(edges[:-1], edges[1:]))
    row = lambda cols, jb=0: pl.BlockSpec((tm, cols), lambda i: (i, jb))
    return pl.pallas_call(
        functools.partial(_merge_ffn_kernel, ff_chunks=ff_chunks),
        out_shape=jax.ShapeDtypeStruct((m, d), F32),
        grid=(m // tm,),
        in_specs=[row(d), row(width), row(width), row(d, 0), row(d, 1),
                  _resident((width, d)), _resident((width, d)), _resident((d, d)), _resident((1, d)),
                  _resident((1, d)), _resident((d, ff)), _resident((d, ff)), _resident((ff, d)),
                  _resident((1, d))],
        out_specs=row(d),
        compiler_params=pltpu.CompilerParams(
            dimension_semantics=("parallel",), vmem_limit_bytes=VMEM_LIMIT),
        name="merge_ffn",
    )(x2d, o_sb, o_rw, gates, gates, w_sb, w_rw, w_out, g_mix, g_pre, w_gate, w_up, w_down, g_post)


def _pick(n, prefs):
    for t in prefs:
        if n % t == 0:
            return t
    return n


def kernel(x, meta_tokens, norm_mix_pre, norm_mix_post, w_in, rw_shift_mu, rw_decay_up, rw_decay_base,
           rw_aaa_up, rw_aaa_base, rw_gate_up, rw_k_k, rw_k_a, rw_r_k, rw_ln_gain, rw_ln_bias,
           w_branch_sb, w_branch_rw, w_out, norm_ffn_pre, norm_ffn_post, w_ffn_gate, w_ffn_up,
           w_ffn_down):
    b, s, d = x.shape
    n_meta = meta_tokens.shape[0]
    depth = w_in.shape[0]
    assert depth == 1, "meta rows are only carried through the mixer of a single layer"
    width = w_branch_sb.shape[1]
    dlora = rw_decay_up.shape[1]
    alora = rw_aaa_up.shape[1]
    glora = rw_gate_up.shape[1]
    assert dlora + alora == LANES and glora == LANES and n_meta <= RW_ROWS
    l = 0
    m = b * s
    x2d = x.reshape(m, d)

    c_sb, c_rw = 3 * width, 3 * width + 3 * width + 2 * LANES
    w_in_b = w_in[l].astype(BF16)
    g_pre = norm_mix_pre[l][None, :]

    qkv, u_rw, gates = _proj(x2d, g_pre, w_in_b, c_sb, c_rw, _pick(m, (1024, 512, 256, 128)), "proj")
    meta_pad = jnp.zeros((LANES, d), F32).at[:n_meta].set(meta_tokens.astype(F32))
    qkv_meta, rw_meta, _ = _proj(meta_pad, g_pre, w_in_b, c_sb, c_rw, LANES, "proj_meta")
    row_ok = (jnp.arange(LANES) < n_meta)[:, None]
    qkv_meta = jnp.where(row_ok, qkv_meta, jnp.zeros_like(qkv_meta))
    u_meta = jnp.zeros((RW_ROWS, c_rw - c_sb), F32).at[RW_ROWS - n_meta:].set(rw_meta[:n_meta])

    tq = _pick(s, (1024, 512, 256, 128))
    o_sb = _sb_attn(qkv.reshape(b, s, c_sb), qkv_meta, n_meta, tq, min(tq, SB_TILE))

    w_lora = jnp.zeros((LANES, 2 * width), F32)
    w_lora = w_lora.at[:dlora, :width].set(rw_decay_up[l]).at[dlora:, width:].set(rw_aaa_up[l])
    vec = lambda p: p[l].reshape(1, -1).astype(F32)
    o_rw = _rwkv(u_rw.reshape(b, s, c_rw - c_sb), u_meta, vec(rw_shift_mu), w_lora,
                 vec(rw_decay_base), vec(rw_aaa_base), rw_gate_up[l].astype(F32), vec(rw_k_k),
                 vec(rw_k_a), vec(rw_r_k), vec(rw_ln_gain), vec(rw_ln_bias))

    out = _merge_ffn(x2d, o_sb.reshape(m, width), o_rw.reshape(m, width), gates,
                     w_branch_sb[l].astype(BF16), w_branch_rw[l].astype(BF16), w_out[l].astype(BF16),
                     norm_mix_post[l][None, :], norm_ffn_pre[l][None, :], w_ffn_gate[l].astype(BF16),
                     w_ffn_up[l].astype(BF16), w_ffn_down[l].astype(BF16), norm_ffn_post[l][None, :],
                     _pick(m, (512, 256, 128)))
    return out.reshape(b, s, d)
```
